```python
import math
import jax, jax.numpy as jnp
from jax import lax
import numpy as np

D_MODEL = 1024
BATCH = 8
SEQ = 8192
DEPTH = 2

EPS = 1e-6
MIX_WIDTH = D_MODEL // 2
HGRN_HEADS = 4
HGRN_DK = MIX_WIDTH // HGRN_HEADS
HGRN_DV = MIX_WIDTH // HGRN_HEADS
HGRN_CHUNK = 64
DIFF_HEADS = 4
DIFF_DQK = MIX_WIDTH // (2 * DIFF_HEADS)
DIFF_DV = MIX_WIDTH // DIFF_HEADS
Q_BLOCK = 128
GMLP_GROUPS = 4
GMLP_DG = MIX_WIDTH // GMLP_GROUPS
GMLP_CHUNK = 128
D_FF = 4 * D_MODEL
N_BRANCH = 3
IN_SIZES = [MIX_WIDTH] * 4 + [MIX_WIDTH] * 3 + [MIX_WIDTH] * 2 + [N_BRANCH * D_MODEL]
IN_COLS = sum(IN_SIZES)

kernel_name = "hybrid_hgrn2_diffattn_gmlp_gated"


def rmsnorm(x, w):
    xf = x.astype(jnp.float32)
    y = xf * lax.rsqrt(jnp.mean(xf * xf, axis=-1, keepdims=True) + EPS)
    return (y * w.astype(jnp.float32)).astype(x.dtype)


def layernorm(x, w, b):
    xf = x.astype(jnp.float32)
    mu = jnp.mean(xf, axis=-1, keepdims=True)
    var = jnp.mean(jnp.square(xf - mu), axis=-1, keepdims=True)
    y = (xf - mu) * lax.rsqrt(var + EPS)
    return (y * w.astype(jnp.float32) + b.astype(jnp.float32)).astype(x.dtype)


def hgrn2_mixer(q, f_logit, i, g, lb, norm_w):
    B, S, _ = q.shape
    dt = q.dtype
    f32 = jnp.float32
    C = HGRN_CHUNK
    nc = S // C
    qf = jax.nn.silu(q.astype(f32))
    lbf = lb.astype(f32)
    log_f = jnp.logaddexp(jnp.log(lbf), jnp.log1p(-lbf) + jax.nn.log_sigmoid(f_logit.astype(f32)))
    kf = -jnp.expm1(log_f)
    vf = i.astype(f32)

    def to_chunks(t, d):
        return t.reshape(B, nc, C, HGRN_HEADS, d).transpose(1, 0, 3, 2, 4)

    qc, kc, lfc = to_chunks(qf, HGRN_DK), to_chunks(kf, HGRN_DK), to_chunks(log_f, HGRN_DK)
    vc = to_chunks(vf, HGRN_DV)
    causal = jnp.tril(jnp.ones((C, C), dtype=bool))

    def step(state, inp):
        qb, kb, lfb, vb = inp
        b = jnp.cumsum(lfb, axis=-2)
        o_inter = jnp.einsum('bhtk,bhkv->bhtv', qb * jnp.exp(b), state)
        rel = b[..., :, None, :] - b[..., None, :, :]
        decay = jnp.exp(jnp.where(causal[:, :, None], rel, -jnp.inf))
        scores = jnp.einsum('bhtk,bhsk,bhtsk->bhts', qb, kb, decay)
        o = o_inter + jnp.einsum('bhts,bhsv->bhtv', scores, vb)
        b_last = b[..., -1:, :]
        k_dec = kb * jnp.exp(b_last - b)
        state = jnp.exp(b_last[..., 0, :])[..., None] * state + jnp.einsum('bhsk,bhsv->bhkv', k_dec, vb)
        return state, o

    state0 = jnp.zeros((B, HGRN_HEADS, HGRN_DK, HGRN_DV), f32)
    _, oc = lax.scan(step, state0, (qc, kc, lfc, vc))
    o = oc.transpose(1, 0, 3, 2, 4).reshape(B, S, HGRN_HEADS, HGRN_DV)
    o = rmsnorm(o, norm_w) * jax.nn.silu(g.astype(f32).reshape(B, S, HGRN_HEADS, HGRN_DV))
    return o.reshape(B, S, MIX_WIDTH).astype(dt)


def diff_attention(q, k, v, lam_q1, lam_k1, lam_q2, lam_k2, lam_init, norm_w):
    B, S, _ = q.shape
    dt = q.dtype
    f32 = jnp.float32
    H, d = DIFF_HEADS, DIFF_DQK
    qf = q.astype(f32).reshape(B, S, H, 2, d)
    kf = k.astype(f32).reshape(B, S, H, 2, d)
    vf = v.astype(f32).reshape(B, S, H, DIFF_DV)
    lam = (jnp.exp(jnp.sum(lam_q1.astype(f32) * lam_k1.astype(f32)))
           - jnp.exp(jnp.sum(lam_q2.astype(f32) * lam_k2.astype(f32))) + lam_init)
    scale = d ** -0.5
    nb = S // Q_BLOCK
    qb = qf.reshape(B, nb, Q_BLOCK, H, 2, d).transpose(1, 0, 2, 3, 4, 5)
    kpos = jnp.arange(S)

    def block(args):
        qblk, blk = args
        s = jnp.einsum('bqhcd,bkhcd->bhcqk', qblk, kf) * scale
        qpos = blk * Q_BLOCK + jnp.arange(Q_BLOCK)
        s = jnp.where(kpos[None, :] <= qpos[:, None], s, -jnp.inf)
        p = jax.nn.softmax(s, axis=-1)
        a = p[:, :, 0] - lam * p[:, :, 1]
        return jnp.einsum('bhqk,bkhv->bqhv', a, vf)

    o = lax.map(block, (qb, jnp.arange(nb)))
    o = o.transpose(1, 0, 2, 3, 4).reshape(B, S, H, DIFF_DV)
    o = rmsnorm(o, norm_w) * (1.0 - lam_init)
    return o.reshape(B, S, MIX_WIDTH).astype(dt)


def chunked_gmlp(u, v, ln_w, ln_b, w_s, b_s):
    B, S, _ = u.shape
    C = GMLP_CHUNK
    nc = S // C
    u = jax.nn.gelu(u, approximate=False)
    v = layernorm(jax.nn.gelu(v, approximate=False), ln_w, ln_b)
    vc = v.reshape(B, nc, C, GMLP_GROUPS, GMLP_DG)
    w = w_s * jnp.tril(jnp.ones((C, C), dtype=w_s.dtype))[None]
    mixed = jnp.einsum('gts,bnsgd->bntgd', w, vc) + b_s.T[:, :, None]
    return u * mixed.reshape(B, S, MIX_WIDTH)


def setup_inputs(seed: int = 0) -> dict:
    key = jax.random.key(seed)
    ks = jax.random.split(key, 24)
    f32 = jnp.float32
    nrm = lambda k, shape, s: jax.random.normal(k, shape, f32) * s
    gain = lambda k, shape: 1.0 + nrm(k, shape, 0.02)
    return {
        "x": nrm(ks[0], (BATCH, SEQ, D_MODEL), 1.0),
        "norm_mix_w": gain(ks[1], (DEPTH, D_MODEL)),
        "w_in": nrm(ks[2], (DEPTH, D_MODEL, IN_COLS), D_MODEL ** -0.5),
        "hgrn_lb_logits": nrm(ks[3], (DEPTH, MIX_WIDTH), 0.5),
        "hgrn_norm_w": gain(ks[4], (DEPTH, HGRN_DV)),
        "diff_lam_q1": nrm(ks[5], (DEPTH, DIFF_DQK), 0.1),
        "diff_lam_k1": nrm(ks[6], (DEPTH, DIFF_DQK), 0.1),
        "diff_lam_q2": nrm(ks[7], (DEPTH, DIFF_DQK), 0.1),
        "diff_lam_k2": nrm(ks[8], (DEPTH, DIFF_DQK), 0.1),
        "diff_norm_w": gain(ks[9], (DEPTH, DIFF_DV)),
        "gmlp_ln_w": gain(ks[10], (DEPTH, MIX_WIDTH)),
        "gmlp_ln_b": nrm(ks[11], (DEPTH, MIX_WIDTH), 0.02),
        "gmlp_w_s": nrm(ks[12], (DEPTH, GMLP_GROUPS, GMLP_CHUNK, GMLP_CHUNK), GMLP_CHUNK ** -0.5),
        "gmlp_b_s": 1.0 + nrm(ks[13], (DEPTH, GMLP_GROUPS, GMLP_CHUNK), 0.1),
        "w_br_hgrn": nrm(ks[14], (DEPTH, MIX_WIDTH, D_MODEL), MIX_WIDTH ** -0.5),
        "w_br_attn": nrm(ks[15], (DEPTH, MIX_WIDTH, D_MODEL), MIX_WIDTH ** -0.5),
        "w_br_gmlp": nrm(ks[16], (DEPTH, MIX_WIDTH, D_MODEL), MIX_WIDTH ** -0.5),
        "w_out": nrm(ks[17], (DEPTH, D_MODEL, D_MODEL), D_MODEL ** -0.5),
        "norm_ff_w": gain(ks[18], (DEPTH, D_MODEL)),
        "w_ff1": nrm(ks[19], (DEPTH, D_MODEL, D_FF), D_MODEL ** -0.5),
        "w_ff2": nrm(ks[20], (DEPTH, D_FF, D_MODEL), D_FF ** -0.5),
        "final_norm_w": gain(ks[21], (D_MODEL,)),
    }


def reference(x, norm_mix_w, w_in, hgrn_lb_logits, hgrn_norm_w, diff_lam_q1, diff_lam_k1,
              diff_lam_q2, diff_lam_k2, diff_norm_w, gmlp_ln_w, gmlp_ln_b, gmlp_w_s, gmlp_b_s,
              w_br_hgrn, w_br_attn, w_br_gmlp, w_out, norm_ff_w, w_ff1, w_ff2, final_norm_w):
    B, S, _ = x.shape
    p = jax.nn.softmax(hgrn_lb_logits.astype(jnp.float32), axis=0)
    cum = jnp.cumsum(p, axis=0)
    lbs = cum - cum[0:1]
    split_idx = [int(v) for v in np.cumsum(IN_SIZES)[:-1]]

    for l in range(DEPTH):
        h = rmsnorm(x, norm_mix_w[l])
        z = jnp.einsum('bsd,dc->bsc', h, w_in[l])
        hq, hf, hi, hg, aq, ak, av, gu, gv, gate_logits = jnp.split(z, split_idx, axis=-1)

        y_h = hgrn2_mixer(hq, hf, hi, hg, lbs[l], hgrn_norm_w[l])
        lam_init = 0.8 - 0.6 * math.exp(-0.3 * l)
        y_a = diff_attention(aq, ak, av, diff_lam_q1[l], diff_lam_k1[l], diff_lam_q2[l],
                             diff_lam_k2[l], lam_init, diff_norm_w[l])
        y_g = chunked_gmlp(gu, gv, gmlp_ln_w[l], gmlp_ln_b[l], gmlp_w_s[l], gmlp_b_s[l])

        gates = jax.nn.sigmoid(gate_logits.astype(jnp.float32)).astype(x.dtype)
        gates = gates.reshape(B, S, N_BRANCH, D_MODEL)
        merged = (gates[:, :, 0] * jnp.einsum('bsm,md->bsd', y_h, w_br_hgrn[l])
                  + gates[:, :, 1] * jnp.einsum('bsm,md->bsd', y_a, w_br_attn[l])
                  + gates[:, :, 2] * jnp.einsum('bsm,md->bsd', y_g, w_br_gmlp[l]))
        x = x + jnp.einsum('bsd,de->bse', merged, w_out[l])

        h2 = rmsnorm(x, norm_ff_w[l])
        ff = jnp.square(jax.nn.relu(jnp.einsum('bsd,df->bsf', h2, w_ff1[l])))
        x = x + jnp.einsum('bsf,fd->bsd', ff, w_ff2[l])

    return rmsnorm(x, final_norm_w)
```

```python
import functools
import math

import jax
import jax.numpy as jnp
from jax import lax
from jax.experimental import pallas as pl
from jax.experimental.pallas import tpu as pltpu

F32 = jnp.float32
BF16 = jnp.bfloat16

D_MODEL = 1024
MIX = 512
HEADS = 4
DH = 128
DQK = 64
D_FF = 4 * D_MODEL
N_BRANCH = 3
EPS = 1e-6
GMLP_CHUNK = 128
LANE = 128
IN_COLS = 9 * MIX + N_BRANCH * D_MODEL
NG = IN_COLS // LANE

G_GATE = 0
G_HQ, G_HF, G_HI, G_HG = 24, 28, 32, 36
G_AQ, G_AK, G_AV = 40, 44, 48
G_GU, G_GV = 52, 56

NEG = -1e30
VMEM_LIMIT = 56 * 1024 * 1024

TM_PROJ = 512
T_HGRN = 256
SUB = 16
TQ = 256
TK = 256
T_GMLP = 256
CW = 512


def _cparams(sem):
    return pltpu.CompilerParams(dimension_semantics=sem, vmem_limit_bytes=VMEM_LIMIT)


def _resident(shape, index_map):
    return pl.BlockSpec(shape, index_map, pipeline_mode=pl.Buffered(1))


def _rms(x, w):
    return x * lax.rsqrt(jnp.mean(x * x, axis=-1, keepdims=True) + EPS) * w


def _gelu(x):
    return 0.5 * x * (1.0 + lax.erf(x * (1.0 / math.sqrt(2.0))))


def _dot(a, b):
    return jnp.dot(a, b, preferred_element_type=F32)


def _dot_nt(a, b):
    return lax.dot_general(a, b, (((1,), (1,)), ((), ())), preferred_element_type=F32)


def _dot_tn(a, b):
    return lax.dot_general(a, b, (((0,), (0,)), ((), ())), preferred_element_type=F32)


def _inproj_kernel(x_ref, nw_ref, w_ref, z_ref, h_ref):
    h_ref[...] = _rms(x_ref[...], nw_ref[...]).astype(BF16)
    gpc = CW // LANE
    for jc in range(IN_COLS // CW):
        r = _dot(h_ref[...], w_ref[:, jc * CW:(jc + 1) * CW])
        for g in range(gpc):
            z_ref[jc * gpc + g] = r[:, g * LANE:(g + 1) * LANE].astype(BF16)


def _inproj(x2, nw, w):
    n = x2.shape[0]
    tm = TM_PROJ
    return pl.pallas_call(
        _inproj_kernel,
        grid=(n // tm,),
        in_specs=[
            pl.BlockSpec((tm, D_MODEL), lambda i: (i, 0)),
            _resident((1, D_MODEL), lambda i: (0, 0)),
            _resident((D_MODEL, IN_COLS), lambda i: (0, 0)),
        ],
        out_specs=pl.BlockSpec((NG, tm, LANE), lambda i: (0, i, 0)),
        out_shape=jax.ShapeDtypeStruct((NG, n, LANE), BF16),
        scratch_shapes=[pltpu.VMEM((tm, D_MODEL), BF16)],
        compiler_params=_cparams(("parallel",)),
        name="inproj",
    )(x2, nw, w)


def _hgrn_kernel(q_ref, f_ref, i_ref, g_ref, llb_ref, l1m_ref, omlb_ref, nw_ref, o_ref, st_ref):
    t_rows = q_ref.shape[1]

    @pl.when(pl.program_id(2) == 0)
    def _():
        st_ref[...] = jnp.zeros_like(st_ref)

    zq = q_ref[0].astype(F32)
    zf = f_ref[0].astype(F32)
    vi = i_ref[0].astype(F32)
    zg = g_ref[0].astype(F32)

    qf = zq * jax.nn.sigmoid(zq)
    log_sig = jnp.minimum(zf, 0.0) - jnp.log1p(jnp.exp(-jnp.abs(zf)))
    c = l1m_ref[...] + log_sig
    a = llb_ref[...]
    logf = jnp.maximum(a, c) + jnp.log1p(jnp.exp(-jnp.abs(a - c)))
    kf = omlb_ref[...] * jax.nn.sigmoid(-zf)

    row = lax.broadcasted_iota(jnp.int32, (t_rows, t_rows), 0)
    col = lax.broadcasted_iota(jnp.int32, (t_rows, t_rows), 1)

    tri = jnp.where(row >= col, 1.0, 0.0).astype(BF16)
    hi = logf.astype(BF16)
    r1 = logf - hi.astype(F32)
    mid = r1.astype(BF16)
    lo = (r1 - mid.astype(F32)).astype(BF16)
    b = _dot(tri, hi) + _dot(tri, mid) + _dot(tri, lo)

    vb = vi.astype(BF16)

    xor = row ^ col
    a_tot = jnp.zeros((t_rows, t_rows), F32)
    h = SUB
    while h < t_rows:
        nb = t_rows // (2 * h)
        b3 = b.reshape(nb, 2 * h, DH)
        rb = jnp.broadcast_to(b3[:, h:h + 1, :], (nb, 2 * h, DH)).reshape(t_rows, DH)
        qp = (qf * jnp.exp(jnp.minimum(b - rb, 0.0))).astype(BF16)
        kp = (kf * jnp.exp(jnp.minimum(rb - b, 0.0))).astype(BF16)
        a_l = _dot_nt(qp, kp)
        shift = h.bit_length() - 1
        a_tot = jnp.where(((xor >> shift) == 1) & (row > col), a_l, a_tot)
        h *= 2
    o = _dot(a_tot.astype(BF16), vb)

    nb = t_rows // SUB
    q3 = qf.reshape(nb, SUB, DH)
    k3 = kf.reshape(nb, SUB, DH)
    b3 = b.reshape(nb, SUB, DH)
    v3 = vi.reshape(nb, SUB, DH)
    trow = lax.broadcasted_iota(jnp.int32, (nb, SUB, DH), 1)
    o3 = jnp.zeros((nb, SUB, DH), F32)
    for s in range(SUB):
        e = jnp.exp(jnp.where(trow >= s, b3 - b3[:, s:s + 1, :], NEG))
        x = q3 * k3[:, s:s + 1, :] * e
        o3 = o3 + jnp.sum(x, axis=-1, keepdims=True) * v3[:, s:s + 1, :]
    o = o + o3.reshape(t_rows, DH)

    st = st_ref[...]
    o = o + _dot_nt((qf * jnp.exp(b)).astype(BF16), st.astype(BF16))
    b_last = b[t_rows - 1:t_rows, :]
    kd = (kf * jnp.exp(b_last - b)).astype(BF16)
    st_ref[...] = jnp.exp(b_last) * st + _dot_tn(vb, kd)

    y = _rms(o, nw_ref[...]) * (zg * jax.nn.sigmoid(zg))
    o_ref[0] = y.astype(BF16)


def _hgrn(z, llb, l1m, omlb, nw, batch, seq):
    t = T_HGRN
    nc = seq // t
    n = batch * seq

    def zspec(g0):
        return pl.BlockSpec((1, t, LANE), lambda b, h, c: (g0 + h, b * nc + c, 0))

    def pspec():
        return pl.BlockSpec((1, LANE), lambda b, h, c: (0, h))

    return pl.pallas_call(
        _hgrn_kernel,
        grid=(batch, HEADS, nc),
        in_specs=[zspec(G_HQ), zspec(G_HF), zspec(G_HI), zspec(G_HG),
                  pspec(), pspec(), pspec(),
                  pl.BlockSpec((1, LANE), lambda b, h, c: (0, 0))],
        out_specs=pl.BlockSpec((1, t, LANE), lambda b, h, c: (h, b * nc + c, 0)),
        out_shape=jax.ShapeDtypeStruct((HEADS, n, LANE), BF16),
        scratch_shapes=[pltpu.VMEM((DH, DH), F32)],
        compiler_params=_cparams(("parallel", "parallel", "arbitrary")),
        name="hgrn2",
    )(z, z, z, z, llb, l1m, omlb, nw)


def _attn_kernel(lam_ref, q_ref, k_ref, v_ref, nw_ref, o_ref, m_ref, l_ref, acc_ref, *, lam_init):
    qi = pl.program_id(2)
    tq = q_ref.shape[1]

    lp = lam_ref[...]
    lam = (jnp.exp(jnp.sum(lp[0:1] * lp[1:2], axis=-1, keepdims=True))
           - jnp.exp(jnp.sum(lp[2:3] * lp[3:4], axis=-1, keepdims=True)) + lam_init)

    lane = lax.broadcasted_iota(jnp.int32, (tq, LANE), 1)
    qs = q_ref[0] * jnp.asarray(DQK ** -0.5, BF16)
    zero = jnp.zeros_like(qs)
    qm = (jnp.where(lane < DQK, qs, zero), jnp.where(lane >= DQK, qs, zero))

    m_ref[...] = jnp.full(m_ref.shape, NEG, F32)
    l_ref[...] = jnp.zeros(l_ref.shape, F32)
    acc_ref[...] = jnp.zeros(acc_ref.shape, F32)

    row = lax.broadcasted_iota(jnp.int32, (tq, TK), 0)
    col = lax.broadcasted_iota(jnp.int32, (tq, TK), 1)

    def tile(j, masked):
        off = pl.multiple_of(j * TK, TK)
        kt = k_ref[0, pl.ds(off, TK), :]
        vt = v_ref[0, pl.ds(off, TK), :]
        for c in range(2):
            s = _dot_nt(qm[c], kt)
            if masked:
                s = jnp.where(col <= row, s, NEG)
            m_prev = m_ref[c]
            m_cur = jnp.maximum(m_prev, jnp.max(s, axis=-1, keepdims=True))
            alpha = jnp.exp(m_prev - m_cur)
            p = jnp.exp(s - m_cur)
            l_ref[c] = alpha * l_ref[c] + jnp.sum(p, axis=-1, keepdims=True)
            acc_ref[c] = alpha * acc_ref[c] + _dot(p.astype(BF16), vt)
            m_ref[c] = m_cur

    def body(j, carry):
        tile(j, False)
        return carry

    lax.fori_loop(0, qi, body, 0)
    tile(qi, True)

    o = acc_ref[0] / l_ref[0] - lam * (acc_ref[1] / l_ref[1])
    o_ref[0] = (_rms(o, nw_ref[...]) * (1.0 - lam_init)).astype(BF16)


def _attn(z, lam_params, nw, lam_init, batch, seq):
    assert TQ == TK
    nq = seq // TQ
    n = batch * seq
    return pl.pallas_call(
        functools.partial(_attn_kernel, lam_init=lam_init),
        grid=(batch, HEADS, nq),
        in_specs=[
            pl.BlockSpec((4, DQK), lambda b, h, i: (0, 0)),
            pl.BlockSpec((1, TQ, LANE), lambda b, h, i: (G_AQ + h, b * nq + i, 0)),
            pl.BlockSpec((1, seq, LANE), lambda b, h, i: (G_AK + h, b, 0)),
            pl.BlockSpec((1, seq, LANE), lambda b, h, i: (G_AV + h, b, 0)),
            pl.BlockSpec((1, LANE), lambda b, h, i: (0, 0)),
        ],
        out_specs=pl.BlockSpec((1, TQ, LANE), lambda b, h, i: (h, b * nq + i, 0)),
        out_shape=jax.ShapeDtypeStruct((HEADS, n, LANE), BF16),
        scratch_shapes=[pltpu.VMEM((2, TQ, 1), F32), pltpu.VMEM((2, TQ, 1), F32),
                        pltpu.VMEM((2, TQ, LANE), F32)],
        compiler_params=_cparams(("parallel", "parallel", "arbitrary")),
        name="diff_attn",
    )(lam_params, z, z, z, nw)


def _gmlp_kernel(u_ref, v_ref, lnw_ref, lnb_ref, ws_ref, bs_ref, o_ref):
    t_rows = u_ref.shape[1]
    ng = MIX // LANE
    vs = [_gelu(v_ref[g].astype(F32)) for g in range(ng)]
    mu = sum(jnp.sum(v, axis=-1, keepdims=True) for v in vs) * (1.0 / MIX)
    var = sum(jnp.sum(jnp.square(v - mu), axis=-1, keepdims=True) for v in vs) * (1.0 / MIX)
    rs = lax.rsqrt(var + EPS)
    row = lax.broadcasted_iota(jnp.int32, (GMLP_CHUNK, GMLP_CHUNK), 0)
    col = lax.broadcasted_iota(jnp.int32, (GMLP_CHUNK, GMLP_CHUNK), 1)
    for g in range(ng):
        vn = ((vs[g] - mu) * rs * lnw_ref[g] + lnb_ref[g]).astype(BF16)
        w = jnp.where(row >= col, ws_ref[g], 0.0).astype(BF16)
        for n in range(t_rows // GMLP_CHUNK):
            sl = slice(n * GMLP_CHUNK, (n + 1) * GMLP_CHUNK)
            mixed = _dot(w, vn[sl]) + bs_ref[g]
            o_ref[g, sl, :] = (_gelu(u_ref[g, sl, :].astype(F32)) * mixed).astype(BF16)


def _gmlp(z, lnw, lnb, ws, bs):
    n = z.shape[1]
    t = T_GMLP
    ng = MIX // LANE
    return pl.pallas_call(
        _gmlp_kernel,
        grid=(n // t,),
        in_specs=[
            pl.BlockSpec((ng, t, LANE), lambda i: (G_GU // ng, i, 0)),
            pl.BlockSpec((ng, t, LANE), lambda i: (G_GV // ng, i, 0)),
            pl.BlockSpec((ng, 1, LANE), lambda i: (0, 0, 0)),
            pl.BlockSpec((ng, 1, LANE), lambda i: (0, 0, 0)),
            pl.BlockSpec((ng, GMLP_CHUNK, GMLP_CHUNK), lambda i: (0, 0, 0)),
            pl.BlockSpec((ng, GMLP_CHUNK, 1), lambda i: (0, 0, 0)),
        ],
        out_specs=pl.BlockSpec((ng, t, LANE), lambda i: (0, i, 0)),
        out_shape=jax.ShapeDtypeStruct((ng, n, LANE), BF16),
        compiler_params=_cparams(("parallel",)),
        name="gmlp",
    )(z, z, lnw, lnb, ws, bs)


def _merge_kernel(x_ref, gl_ref, yh_ref, ya_ref, yg_ref, wh_ref, wa_ref, wg_ref, wo_ref, o_ref, m_ref):
    ng = MIX // LANE
    ys = [jnp.concatenate([r[g] for g in range(ng)], axis=-1) for r in (yh_ref, ya_ref, yg_ref)]
    ws = (wh_ref, wa_ref, wg_ref)
    cw = 2 * LANE
    gpb = D_MODEL // LANE
    for cc in range(D_MODEL // cw):
        acc = None
        for br in range(N_BRANCH):
            p = _dot(ys[br], ws[br][:, cc * cw:(cc + 1) * cw])
            g0 = br * gpb + 2 * cc
            gl = jnp.concatenate([gl_ref[g0], gl_ref[g0 + 1]], axis=-1).astype(F32)
            t = jax.nn.sigmoid(gl) * p
            acc = t if acc is None else acc + t
        m_ref[:, cc * cw:(cc + 1) * cw] = acc.astype(BF16)
    o_ref[...] = x_ref[...] + _dot(m_ref[...], wo_ref[...])


def _merge(x2, z, yh, ya, yg, wh, wa, wg, wo):
    n = x2.shape[0]
    tm = TM_PROJ
    ng = MIX // LANE
    ngate = N_BRANCH * D_MODEL // LANE

    def yspec():
        return pl.BlockSpec((ng, tm, LANE), lambda i: (0, i, 0))

    def wspec():
        return _resident((MIX, D_MODEL), lambda i: (0, 0))

    return pl.pallas_call(
        _merge_kernel,
        grid=(n // tm,),
        in_specs=[
            pl.BlockSpec((tm, D_MODEL), lambda i: (i, 0)),
            pl.BlockSpec((ngate, tm, LANE), lambda i: (G_GATE // ngate, i, 0)),
            yspec(), yspec(), yspec(),
            wspec(), wspec(), wspec(),
            _resident((D_MODEL, D_MODEL), lambda i: (0, 0)),
        ],
        out_specs=pl.BlockSpec((tm, D_MODEL), lambda i: (i, 0)),
        out_shape=jax.ShapeDtypeStruct((n, D_MODEL), F32),
        scratch_shapes=[pltpu.VMEM((tm, D_MODEL), BF16)],
        compiler_params=_cparams(("parallel",)),
        name="merge_out",
    )(x2, z, yh, ya, yg, wh, wa, wg, wo)


def _ffn_kernel(x_ref, nw_ref, w1_ref, w2_ref, fw_ref, o_ref, h_ref, ff_ref, *, final):
    x = x_ref[...]
    h_ref[...] = _rms(x, nw_ref[...]).astype(BF16)
    fc = D_MODEL
    for j in range(D_FF // fc):
        a = jnp.maximum(_dot(h_ref[...], w1_ref[:, j * fc:(j + 1) * fc]), 0.0)
        ff_ref[:, j * fc:(j + 1) * fc] = (a * a).astype(BF16)
    y = x + _dot(ff_ref[...], w2_ref[...])
    if final:
        y = _rms(y, fw_ref[...])
    o_ref[...] = y


def _ffn(x2, nw, w1, w2, fw, final):
    n = x2.shape[0]
    tm = TM_PROJ
    return pl.pallas_call(
        functools.partial(_ffn_kernel, final=final),
        grid=(n // tm,),
        in_specs=[
            pl.BlockSpec((tm, D_MODEL), lambda i: (i, 0)),
            _resident((1, D_MODEL), lambda i: (0, 0)),
            _resident((D_MODEL, D_FF), lambda i: (0, 0)),
            _resident((D_FF, D_MODEL), lambda i: (0, 0)),
            _resident((1, D_MODEL), lambda i: (0, 0)),
        ],
        out_specs=pl.BlockSpec((tm, D_MODEL), lambda i: (i, 0)),
        out_shape=jax.ShapeDtypeStruct((n, D_MODEL), F32),
        scratch_shapes=[pltpu.VMEM((tm, D_MODEL), BF16), pltpu.VMEM((tm, D_FF), BF16)],
        compiler_params=_cparams(("parallel",)),
        name="ffn",
    )(x2, nw, w1, w2, fw)


def kernel(x, norm_mix_w, w_in, hgrn_lb_logits, hgrn_norm_w, diff_lam_q1, diff_lam_k1, diff_lam_q2,
           diff_lam_k2, diff_norm_w, gmlp_ln_w, gmlp_ln_b, gmlp_w_s, gmlp_b_s, w_br_hgrn, w_br_attn,
           w_br_gmlp, w_out, norm_ff_w, w_ff1, w_ff2, final_norm_w):
    batch, seq, _ = x.shape
    depth = w_in.shape[0]
    n = batch * seq
    assert n % TM_PROJ == 0 and seq % T_HGRN == 0 and seq % TQ == 0 and n % T_GMLP == 0

    cum = jnp.cumsum(jax.nn.softmax(hgrn_lb_logits.astype(F32), axis=0), axis=0)
    lbs = cum - cum[0:1]
    log_lb = jnp.maximum(jnp.log(lbs), NEG)
    log_1m_lb = jnp.log1p(-lbs)
    one_m_lb = 1.0 - lbs

    split = 9 * MIX
    x2 = x.reshape(n, D_MODEL)
    for l in range(depth):
        w_l = jnp.concatenate([w_in[l][:, split:], w_in[l][:, :split]], axis=1).astype(BF16)
        z = _inproj(x2, norm_mix_w[l].reshape(1, D_MODEL), w_l)

        yh = _hgrn(z, log_lb[l].reshape(1, MIX), log_1m_lb[l].reshape(1, MIX),
                   one_m_lb[l].reshape(1, MIX), hgrn_norm_w[l].reshape(1, DH), batch, seq)
        lam_init = 0.8 - 0.6 * math.exp(-0.3 * l)
        lam_params = jnp.stack([diff_lam_q1[l], diff_lam_k1[l], diff_lam_q2[l], diff_lam_k2[l]]).astype(F32)
        ya = _attn(z, lam_params, diff_norm_w[l].reshape(1, DH), lam_init, batch, seq)
        yg = _gmlp(z, gmlp_ln_w[l].reshape(MIX // LANE, 1, LANE), gmlp_ln_b[l].reshape(MIX // LANE, 1, LANE),
                   gmlp_w_s[l], gmlp_b_s[l][..., None])

        x2 = _merge(x2, z, yh, ya, yg, w_br_hgrn[l].astype(BF16), w_br_attn[l].astype(BF16),
                    w_br_gmlp[l].astype(BF16), w_out[l].astype(BF16))
        x2 = _ffn(x2, norm_ff_w[l].reshape(1, D_MODEL), w_ff1[l].astype(BF16), w_ff2[l].astype(BF16),
                  final_norm_w.reshape(1, D_MODEL), final=(l == depth - 1))
    return x2.reshape(batch, seq, D_MODEL)
```

```python
import functools
import math

import jax
import jax.numpy as jnp
from jax import lax
from jax.experimental import pallas as pl
from jax.experimental.pallas import tpu as pltpu

F32 = jnp.float32
BF16 = jnp.bfloat16

D_MODEL = 1024
MIX = 512
HEADS = 4
DH = 128
DQK = 64
D_FF = 4 * D_MODEL
N_BRANCH = 3
EPS = 1e-6
GMLP_CHUNK = 128
LANE = 128
IN_COLS = 9 * MIX + N_BRANCH * D_MODEL
NG = IN_COLS // LANE

G_GATE = 0
G_HQ, G_HF, G_HI, G_HG = 24, 28, 32, 36
G_AQ, G_AK, G_AV = 40, 44, 48
G_GU, G_GV = 52, 56

LOG2E = 1.4426950408889634
NEG = -1e30
VMEM_LIMIT = 56 * 1024 * 1024

TM_PROJ = 512
T_HGRN = 256
SUB = 16
TQ = 256
TK = 256
TK_BIG = 1024
T_GMLP = 256
CW = 512


def _cparams(sem):
    return pltpu.CompilerParams(dimension_semantics=sem, vmem_limit_bytes=VMEM_LIMIT)


def _resident(shape, index_map):
    return pl.BlockSpec(shape, index_map, pipeline_mode=pl.Buffered(1))


def _rms(x, w):
    return x * lax.rsqrt(jnp.mean(x * x, axis=-1, keepdims=True) + EPS) * w


def _gelu(x):
    return 0.5 * x * (1.0 + lax.erf(x * (1.0 / math.sqrt(2.0))))


def _dot(a, b):
    return jnp.dot(a, b, preferred_element_type=F32)


def _dot_nt(a, b):
    return lax.dot_general(a, b, (((1,), (1,)), ((), ())), preferred_element_type=F32)


def _dot_tn(a, b):
    return lax.dot_general(a, b, (((0,), (0,)), ((), ())), preferred_element_type=F32)


def _inproj_kernel(x_ref, nw_ref, w_ref, z_ref, h_ref):
    h_ref[...] = _rms(x_ref[...], nw_ref[...]).astype(BF16)
    gpc = CW // LANE
    for jc in range(IN_COLS // CW):
        r = _dot(h_ref[...], w_ref[:, jc * CW:(jc + 1) * CW])
        for g in range(gpc):
            z_ref[jc * gpc + g] = r[:, g * LANE:(g + 1) * LANE].astype(BF16)


def _inproj(x2, nw, w):
    n = x2.shape[0]
    tm = TM_PROJ
    return pl.pallas_call(
        _inproj_kernel,
        grid=(n // tm,),
        in_specs=[
            pl.BlockSpec((tm, D_MODEL), lambda i: (i, 0)),
            _resident((1, D_MODEL), lambda i: (0, 0)),
            _resident((D_MODEL, IN_COLS), lambda i: (0, 0)),
        ],
        out_specs=pl.BlockSpec((NG, tm, LANE), lambda i: (0, i, 0)),
        out_shape=jax.ShapeDtypeStruct((NG, n, LANE), BF16),
        scratch_shapes=[pltpu.VMEM((tm, D_MODEL), BF16)],
        compiler_params=_cparams(("parallel",)),
        name="inproj",
    )(x2, nw, w)


def _hgrn_kernel(q_ref, f_ref, i_ref, g_ref, llb_ref, l1m_ref, omlb_ref, nw_ref, o_ref, st_ref):
    t_rows = q_ref.shape[1]

    @pl.when(pl.program_id(2) == 0)
    def _():
        st_ref[...] = jnp.zeros_like(st_ref)

    zq = q_ref[0].astype(F32)
    zf = f_ref[0].astype(F32)
    vi = i_ref[0].astype(F32)
    zg = g_ref[0].astype(F32)

    qf = zq * jax.nn.sigmoid(zq)
    log_sig = jnp.minimum(zf, 0.0) - jnp.log1p(jnp.exp(-jnp.abs(zf)))
    c = l1m_ref[...] + log_sig
    a = llb_ref[...]
    logf = jnp.maximum(a, c) + jnp.log1p(jnp.exp(-jnp.abs(a - c)))
    kf = omlb_ref[...] * jax.nn.sigmoid(-zf)

    row = lax.broadcasted_iota(jnp.int32, (t_rows, t_rows), 0)
    col = lax.broadcasted_iota(jnp.int32, (t_rows, t_rows), 1)

    tri = jnp.where(row >= col, 1.0, 0.0).astype(BF16)
    hi = logf.astype(BF16)
    r1 = logf - hi.astype(F32)
    mid = r1.astype(BF16)
    lo = (r1 - mid.astype(F32)).astype(BF16)
    b = _dot(tri, hi) + _dot(tri, mid) + _dot(tri, lo)

    vb = vi.astype(BF16)

    xor = row ^ col
    a_tot = jnp.zeros((t_rows, t_rows), F32)
    h = SUB
    while h < t_rows:
        nb = t_rows // (2 * h)
        b3 = b.reshape(nb, 2 * h, DH)
        rb = jnp.broadcast_to(b3[:, h:h + 1, :], (nb, 2 * h, DH)).reshape(t_rows, DH)
        qp = (qf * jnp.exp(jnp.minimum(b - rb, 0.0))).astype(BF16)
        kp = (kf * jnp.exp(jnp.minimum(rb - b, 0.0))).astype(BF16)
        a_l = _dot_nt(qp, kp)
        shift = h.bit_length() - 1
        a_tot = jnp.where(((xor >> shift) == 1) & (row > col), a_l, a_tot)
        h *= 2
    o = _dot(a_tot.astype(BF16), vb)

    nb = t_rows // SUB
    q3 = qf.reshape(nb, SUB, DH)
    k3 = kf.reshape(nb, SUB, DH)
    b3 = b.reshape(nb, SUB, DH)
    v3 = vi.reshape(nb, SUB, DH)
    trow = lax.broadcasted_iota(jnp.int32, (nb, SUB, DH), 1)
    o3 = jnp.zeros((nb, SUB, DH), F32)
    for s in range(SUB):
        e = jnp.exp(jnp.where(trow >= s, b3 - b3[:, s:s + 1, :], NEG))
        x = q3 * k3[:, s:s + 1, :] * e
        o3 = o3 + jnp.sum(x, axis=-1, keepdims=True) * v3[:, s:s + 1, :]
    o = o + o3.reshape(t_rows, DH)

    st = st_ref[...]
    o = o + _dot_nt((qf * jnp.exp(b)).astype(BF16), st.astype(BF16))
    b_last = b[t_rows - 1:t_rows, :]
    kd = (kf * jnp.exp(b_last - b)).astype(BF16)
    st_ref[...] = jnp.exp(b_last) * st + _dot_tn(vb, kd)

    y = _rms(o, nw_ref[...]) * (zg * jax.nn.sigmoid(zg))
    o_ref[0] = y.astype(BF16)


def _hgrn(z, llb, l1m, omlb, nw, batch, seq):
    t = T_HGRN
    nc = seq // t
    n = batch * seq

    def zspec(g0):
        return pl.BlockSpec((1, t, LANE), lambda b, h, c: (g0 + h, b * nc + c, 0))

    def pspec():
        return pl.BlockSpec((1, LANE), lambda b, h, c: (0, h))

    return pl.pallas_call(
        _hgrn_kernel,
        grid=(batch, HEADS, nc),
        in_specs=[zspec(G_HQ), zspec(G_HF), zspec(G_HI), zspec(G_HG),
                  pspec(), pspec(), pspec(),
                  pl.BlockSpec((1, LANE), lambda b, h, c: (0, 0))],
        out_specs=pl.BlockSpec((1, t, LANE), lambda b, h, c: (h, b * nc + c, 0)),
        out_shape=jax.ShapeDtypeStruct((HEADS, n, LANE), BF16),
        scratch_shapes=[pltpu.VMEM((DH, DH), F32)],
        compiler_params=_cparams(("parallel", "parallel", "arbitrary")),
        name="hgrn2",
    )(z, z, z, z, llb, l1m, omlb, nw)


def _attn_kernel(lam_ref, q_ref, k_ref, v_ref, nw_ref, o_ref, m_ref, l_ref, acc_ref, *, lam_init):
    qi = pl.program_id(2)
    tq = q_ref.shape[1]

    lp = lam_ref[...]
    lam = (jnp.exp(jnp.sum(lp[0:1] * lp[1:2], axis=-1, keepdims=True))
           - jnp.exp(jnp.sum(lp[2:3] * lp[3:4], axis=-1, keepdims=True)) + lam_init)

    qt = q_ref[0].astype(F32).T * (DQK ** -0.5 * LOG2E)
    sub = lax.broadcasted_iota(jnp.int32, (LANE, tq), 0)
    qts = jnp.concatenate([jnp.where(sub < DQK, qt, 0.0), jnp.where(sub >= DQK, qt, 0.0)],
                          axis=1).astype(BF16)

    m_ref[...] = jnp.full(m_ref.shape, NEG, F32)
    l_ref[...] = jnp.zeros(l_ref.shape, F32)
    acc_ref[...] = jnp.zeros(acc_ref.shape, F32)

    def tile(off, tk, masked):
        kt = k_ref[0, pl.ds(off, tk), :]
        vt = v_ref[0, pl.ds(off, tk), :]
        s = _dot(kt, qts)
        if masked:
            krow = lax.broadcasted_iota(jnp.int32, (tk, 2 * tq), 0)
            qcol = lax.broadcasted_iota(jnp.int32, (tk, 2 * tq), 1) & (tq - 1)
            s = jnp.where(krow <= qcol, s, NEG)
        m_prev = m_ref[...]
        m_cur = jnp.maximum(m_prev, jnp.max(s, axis=0, keepdims=True))
        alpha = jnp.exp2(m_prev - m_cur)
        p = jnp.exp2(s - m_cur)
        l_ref[...] = alpha * l_ref[...] + jnp.sum(p, axis=0, keepdims=True)
        acc_ref[...] = alpha * acc_ref[...] + _dot_tn(vt, p.astype(BF16))
        m_ref[...] = m_cur

    ratio = TK_BIG // TK
    n_big = qi // ratio

    def big_body(j, carry):
        tile(pl.multiple_of(j * TK_BIG, TK_BIG), TK_BIG, False)
        return carry

    def small_body(j, carry):
        tile(pl.multiple_of(j * TK, TK), TK, False)
        return carry

    lax.fori_loop(0, n_big, big_body, 0)
    lax.fori_loop(n_big * ratio, qi, small_body, 0)
    tile(pl.multiple_of(qi * TK, TK), TK, True)

    on = acc_ref[...] / l_ref[...]
    ot = on[:, :tq] - lam * on[:, tq:]
    yt = ot * lax.rsqrt(jnp.mean(ot * ot, axis=0, keepdims=True) + EPS) * nw_ref[...]
    o_ref[0] = (yt.T * (1.0 - lam_init)).astype(BF16)


def _attn(z, lam_params, nw, lam_init, batch, seq):
    assert TQ == TK
    nq = seq // TQ
    n = batch * seq
    return pl.pallas_call(
        functools.partial(_attn_kernel, lam_init=lam_init),
        grid=(batch, HEADS, nq),
        in_specs=[
            pl.BlockSpec((4, DQK), lambda b, h, i: (0, 0)),
            pl.BlockSpec((1, TQ, LANE), lambda b, h, i: (G_AQ + h, b * nq + i, 0)),
            pl.BlockSpec((1, seq, LANE), lambda b, h, i: (G_AK + h, b, 0)),
            pl.BlockSpec((1, seq, LANE), lambda b, h, i: (G_AV + h, b, 0)),
            pl.BlockSpec((DH, 1), lambda b, h, i: (0, 0)),
        ],
        out_specs=pl.BlockSpec((1, TQ, LANE), lambda b, h, i: (h, b * nq + i, 0)),
        out_shape=jax.ShapeDtypeStruct((HEADS, n, LANE), BF16),
        scratch_shapes=[pltpu.VMEM((1, 2 * TQ), F32), pltpu.VMEM((1, 2 * TQ), F32),
                        pltpu.VMEM((DH, 2 * TQ), F32)],
        compiler_params=_cparams(("parallel", "parallel", "arbitrary")),
        name="diff_attn",
    )(lam_params, z, z, z, nw)


def _gmlp_kernel(u_ref, v_ref, lnw_ref, lnb_ref, ws_ref, bs_ref, o_ref):
    t_rows = u_ref.shape[1]
    ng = MIX // LANE
    vs = [_gelu(v_ref[g].astype(F32)) for g in range(ng)]
    mu = sum(jnp.sum(v, axis=-1, keepdims=True) for v in vs) * (1.0 / MIX)
    var = sum(jnp.sum(jnp.square(v - mu), axis=-1, keepdims=True) for v in vs) * (1.0 / MIX)
    rs = lax.rsqrt(var + EPS)
    row = lax.broadcasted_iota(jnp.int32, (GMLP_CHUNK, GMLP_CHUNK), 0)
    col = lax.broadcasted_iota(jnp.int32, (GMLP_CHUNK, GMLP_CHUNK), 1)
    for g in range(ng):
        vn = ((vs[g] - mu) * rs * lnw_ref[g] + lnb_ref[g]).astype(BF16)
        w = jnp.where(row >= col, ws_ref[g], 0.0).astype(BF16)
        for n in range(t_rows // GMLP_CHUNK):
            sl = slice(n * GMLP_CHUNK, (n + 1) * GMLP_CHUNK)
            mixed = _dot(w, vn[sl]) + bs_ref[g]
            o_ref[g, sl, :] = (_gelu(u_ref[g, sl, :].astype(F32)) * mixed).astype(BF16)


def _gmlp(z, lnw, lnb, ws, bs):
    n = z.shape[1]
    t = T_GMLP
    ng = MIX // LANE
    return pl.pallas_call(
        _gmlp_kernel,
        grid=(n // t,),
        in_specs=[
            pl.BlockSpec((ng, t, LANE), lambda i: (G_GU // ng, i, 0)),
            pl.BlockSpec((ng, t, LANE), lambda i: (G_GV // ng, i, 0)),
            pl.BlockSpec((ng, 1, LANE), lambda i: (0, 0, 0)),
            pl.BlockSpec((ng, 1, LANE), lambda i: (0, 0, 0)),
            pl.BlockSpec((ng, GMLP_CHUNK, GMLP_CHUNK), lambda i: (0, 0, 0)),
            pl.BlockSpec((ng, GMLP_CHUNK, 1), lambda i: (0, 0, 0)),
        ],
        out_specs=pl.BlockSpec((ng, t, LANE), lambda i: (0, i, 0)),
        out_shape=jax.ShapeDtypeStruct((ng, n, LANE), BF16),
        compiler_params=_cparams(("parallel",)),
        name="gmlp",
    )(z, z, lnw, lnb, ws, bs)


def _merge_kernel(x_ref, gl_ref, yh_ref, ya_ref, yg_ref, wh_ref, wa_ref, wg_ref, wo_ref, o_ref, m_ref):
    ng = MIX // LANE
    ys = [jnp.concatenate([r[g] for g in range(ng)], axis=-1) for r in (yh_ref, ya_ref, yg_ref)]
    ws = (wh_ref, wa_ref, wg_ref)
    cw = 2 * LANE
    gpb = D_MODEL // LANE
    for cc in range(D_MODEL // cw):
        acc = None
        for br in range(N_BRANCH):
            p = _dot(ys[br], ws[br][:, cc * cw:(cc + 1) * cw])
            g0 = br * gpb + 2 * cc
            gl = jnp.concatenate([gl_ref[g0], gl_ref[g0 + 1]], axis=-1).astype(F32)
            t = jax.nn.sigmoid(gl) * p
            acc = t if acc is None else acc + t
        m_ref[:, cc * cw:(cc + 1) * cw] = acc.astype(BF16)
    o_ref[...] = x_ref[...] + _dot(m_ref[...], wo_ref[...])


def _merge(x2, z, yh, ya, yg, wh, wa, wg, wo):
    n = x2.shape[0]
    tm = TM_PROJ
    ng = MIX // LANE
    ngate = N_BRANCH * D_MODEL // LANE

    def yspec():
        return pl.BlockSpec((ng, tm, LANE), lambda i: (0, i, 0))

    def wspec():
        return _resident((MIX, D_MODEL), lambda i: (0, 0))

    return pl.pallas_call(
        _merge_kernel,
        grid=(n // tm,),
        in_specs=[
            pl.BlockSpec((tm, D_MODEL), lambda i: (i, 0)),
            pl.BlockSpec((ngate, tm, LANE), lambda i: (G_GATE // ngate, i, 0)),
            yspec(), yspec(), yspec(),
            wspec(), wspec(), wspec(),
            _resident((D_MODEL, D_MODEL), lambda i: (0, 0)),
        ],
        out_specs=pl.BlockSpec((tm, D_MODEL), lambda i: (i, 0)),
        out_shape=jax.ShapeDtypeStruct((n, D_MODEL), F32),
        scratch_shapes=[pltpu.VMEM((tm, D_MODEL), BF16)],
        compiler_params=_cparams(("parallel",)),
        name="merge_out",
    )(x2, z, yh, ya, yg, wh, wa, wg, wo)


def _ffn_kernel(x_ref, nw_ref, w1_ref, w2_ref, fw_ref, o_ref, h_ref, ff_ref, *, final):
    x = x_ref[...]
    h_ref[...] = _rms(x, nw_ref[...]).astype(BF16)
    fc = D_MODEL
    for j in range(D_FF // fc):
        a = jnp.maximum(_dot(h_ref[...], w1_ref[:, j * fc:(j + 1) * fc]), 0.0)
        ff_ref[:, j * fc:(j + 1) * fc] = (a * a).astype(BF16)
    y = x + _dot(ff_ref[...], w2_ref[...])
    if final:
        y = _rms(y, fw_ref[...])
    o_ref[...] = y


def _ffn(x2, nw, w1, w2, fw, final):
    n = x2.shape[0]
    tm = TM_PROJ
    return pl.pallas_call(
        functools.partial(_ffn_kernel, final=final),
        grid=(n // tm,),
        in_specs=[
            pl.BlockSpec((tm, D_MODEL), lambda i: (i, 0)),
            _resident((1, D_MODEL), lambda i: (0, 0)),
            _resident((D_MODEL, D_FF), lambda i: (0, 0)),
            _resident((D_FF, D_MODEL), lambda i: (0, 0)),
            _resident((1, D_MODEL), lambda i: (0, 0)),
        ],
        out_specs=pl.BlockSpec((tm, D_MODEL), lambda i: (i, 0)),
        out_shape=jax.ShapeDtypeStruct((n, D_MODEL), F32),
        scratch_shapes=[pltpu.VMEM((tm, D_MODEL), BF16), pltpu.VMEM((tm, D_FF), BF16)],
        compiler_params=_cparams(("parallel",)),
        name="ffn",
    )(x2, nw, w1, w2, fw)


def kernel(x, norm_mix_w, w_in, hgrn_lb_logits, hgrn_norm_w, diff_lam_q1, diff_lam_k1, diff_lam_q2,
           diff_lam_k2, diff_norm_w, gmlp_ln_w, gmlp_ln_b, gmlp_w_s, gmlp_b_s, w_br_hgrn, w_br_attn,
           w_br_gmlp, w_out, norm_ff_w, w_ff1, w_ff2, final_norm_w):
    batch, seq, _ = x.shape
    depth = w_in.shape[0]
    n = batch * seq
    assert n % TM_PROJ == 0 and seq % T_HGRN == 0 and seq % TQ == 0 and n % T_GMLP == 0

    cum = jnp.cumsum(jax.nn.softmax(hgrn_lb_logits.astype(F32), axis=0), axis=0)
    lbs = cum - cum[0:1]
    log_lb = jnp.maximum(jnp.log(lbs), NEG)
    log_1m_lb = jnp.log1p(-lbs)
    one_m_lb = 1.0 - lbs

    split = 9 * MIX
    x2 = x.reshape(n, D_MODEL)
    for l in range(depth):
        w_l = jnp.concatenate([w_in[l][:, split:], w_in[l][:, :split]], axis=1).astype(BF16)
        z = _inproj(x2, norm_mix_w[l].reshape(1, D_MODEL), w_l)

        yh = _hgrn(z, log_lb[l].reshape(1, MIX), log_1m_lb[l].reshape(1, MIX),
                   one_m_lb[l].reshape(1, MIX), hgrn_norm_w[l].reshape(1, DH), batch, seq)
        lam_init = 0.8 - 0.6 * math.exp(-0.3 * l)
        lam_params = jnp.stack([diff_lam_q1[l], diff_lam_k1[l], diff_lam_q2[l], diff_lam_k2[l]]).astype(F32)
        ya = _attn(z, lam_params, diff_norm_w[l].reshape(DH, 1), lam_init, batch, seq)
        yg = _gmlp(z, gmlp_ln_w[l].reshape(MIX // LANE, 1, LANE), gmlp_ln_b[l].reshape(MIX // LANE, 1, LANE),
                   gmlp_w_s[l], gmlp_b_s[l][..., None])

        x2 = _merge(x2, z, yh, ya, yg, w_br_hgrn[l].astype(BF16), w_br_attn[l].astype(BF16),
                    w_br_gmlp[l].astype(BF16), w_out[l].astype(BF16))
        x2 = _ffn(x2, norm_ff_w[l].reshape(1, D_MODEL), w_ff1[l].astype(BF16), w_ff2[l].astype(BF16),
                  final_norm_w.reshape(1, D_MODEL), final=(l == depth - 1))
    return x2.reshape(batch, seq, D_MODEL)
```

```python
import functools
import math

import jax
import jax.numpy as jnp
from jax import lax
from jax.experimental import pallas as pl
from jax.experimental.pallas import tpu as pltpu

F32 = jnp.float32
BF16 = jnp.bfloat16

D_MODEL = 1024
MIX = 512
HEADS = 4
DH = 128
DQK = 64
D_FF = 4 * D_MODEL
N_BRANCH = 3
EPS = 1e-6
GMLP_CHUNK = 128
LANE = 128
IN_COLS = 9 * MIX + N_BRANCH * D_MODEL
NG = IN_COLS // LANE

G_GATE = 0
G_HQ, G_HF, G_HI, G_HG = 24, 28, 32, 36
G_AQ, G_AK, G_AV = 40, 44, 48
G_GU, G_GV = 52, 56

LOG2E = 1.4426950408889634
NEG = -1e30
VMEM_LIMIT = 56 * 1024 * 1024

TM_PROJ = 512
T_HGRN = 256
SUB = 16
TQ = 512
TK = 512
T_GMLP = 256
CW = 512


def _cparams(sem):
    return pltpu.CompilerParams(dimension_semantics=sem, vmem_limit_bytes=VMEM_LIMIT)


def _resident(shape, index_map):
    return pl.BlockSpec(shape, index_map, pipeline_mode=pl.Buffered(1))


def _rms(x, w):
    return x * lax.rsqrt(jnp.mean(x * x, axis=-1, keepdims=True) + EPS) * w


def _gelu(x):
    return 0.5 * x * (1.0 + lax.erf(x * (1.0 / math.sqrt(2.0))))


def _dot(a, b):
    return jnp.dot(a, b, preferred_element_type=F32)


def _dot_nt(a, b):
    return lax.dot_general(a, b, (((1,), (1,)), ((), ())), preferred_element_type=F32)


def _dot_tn(a, b):
    return lax.dot_general(a, b, (((0,), (0,)), ((), ())), preferred_element_type=F32)


def _inproj_kernel(x_ref, nw_ref, w_ref, z_ref, h_ref):
    h_ref[...] = _rms(x_ref[...], nw_ref[...]).astype(BF16)
    gpc = CW // LANE
    for jc in range(IN_COLS // CW):
        r = _dot(h_ref[...], w_ref[:, jc * CW:(jc + 1) * CW])
        for g in range(gpc):
            z_ref[jc * gpc + g] = r[:, g * LANE:(g + 1) * LANE].astype(BF16)


def _inproj(x2, nw, w):
    n = x2.shape[0]
    tm = TM_PROJ
    return pl.pallas_call(
        _inproj_kernel,
        grid=(n // tm,),
        in_specs=[
            pl.BlockSpec((tm, D_MODEL), lambda i: (i, 0)),
            _resident((1, D_MODEL), lambda i: (0, 0)),
            _resident((D_MODEL, IN_COLS), lambda i: (0, 0)),
        ],
        out_specs=pl.BlockSpec((NG, tm, LANE), lambda i: (0, i, 0)),
        out_shape=jax.ShapeDtypeStruct((NG, n, LANE), BF16),
        scratch_shapes=[pltpu.VMEM((tm, D_MODEL), BF16)],
        compiler_params=_cparams(("parallel",)),
        name="inproj",
    )(x2, nw, w)


def _hgrn_kernel(q_ref, f_ref, i_ref, g_ref, llb_ref, l1m_ref, omlb_ref, nw_ref, o_ref, st_ref):
    t_rows = q_ref.shape[1]

    @pl.when(pl.program_id(2) == 0)
    def _():
        st_ref[...] = jnp.zeros_like(st_ref)

    zq = q_ref[0].astype(F32)
    zf = f_ref[0].astype(F32)
    vi = i_ref[0].astype(F32)
    zg = g_ref[0].astype(F32)

    qf = zq * jax.nn.sigmoid(zq)
    log_sig = jnp.minimum(zf, 0.0) - jnp.log1p(jnp.exp(-jnp.abs(zf)))
    c = l1m_ref[...] + log_sig
    a = llb_ref[...]
    logf = jnp.maximum(a, c) + jnp.log1p(jnp.exp(-jnp.abs(a - c)))
    kf = omlb_ref[...] * jax.nn.sigmoid(-zf)

    row = lax.broadcasted_iota(jnp.int32, (t_rows, t_rows), 0)
    col = lax.broadcasted_iota(jnp.int32, (t_rows, t_rows), 1)

    tri = jnp.where(row >= col, 1.0, 0.0).astype(BF16)
    hi = logf.astype(BF16)
    r1 = logf - hi.astype(F32)
    mid = r1.astype(BF16)
    lo = (r1 - mid.astype(F32)).astype(BF16)
    b = _dot(tri, hi) + _dot(tri, mid) + _dot(tri, lo)

    vb = vi.astype(BF16)

    xor = row ^ col
    a_tot = jnp.zeros((t_rows, t_rows), F32)
    h = SUB
    while h < t_rows:
        nb = t_rows // (2 * h)
        b3 = b.reshape(nb, 2 * h, DH)
        rb = jnp.broadcast_to(b3[:, h:h + 1, :], (nb, 2 * h, DH)).reshape(t_rows, DH)
        qp = (qf * jnp.exp(jnp.minimum(b - rb, 0.0))).astype(BF16)
        kp = (kf * jnp.exp(jnp.minimum(rb - b, 0.0))).astype(BF16)
        a_l = _dot_nt(qp, kp)
        shift = h.bit_length() - 1
        a_tot = jnp.where(((xor >> shift) == 1) & (row > col), a_l, a_tot)
        h *= 2
    o = _dot(a_tot.astype(BF16), vb)

    nb = t_rows // SUB
    q3 = qf.reshape(nb, SUB, DH)
    k3 = kf.reshape(nb, SUB, DH)
    b3 = b.reshape(nb, SUB, DH)
    v3 = vi.reshape(nb, SUB, DH)
    trow = lax.broadcasted_iota(jnp.int32, (nb, SUB, DH), 1)
    o3 = jnp.zeros((nb, SUB, DH), F32)
    for s in range(SUB):
        e = jnp.exp(jnp.where(trow >= s, b3 - b3[:, s:s + 1, :], NEG))
        x = q3 * k3[:, s:s + 1, :] * e
        o3 = o3 + jnp.sum(x, axis=-1, keepdims=True) * v3[:, s:s + 1, :]
    o = o + o3.reshape(t_rows, DH)

    st = st_ref[...]
    o = o + _dot_nt((qf * jnp.exp(b)).astype(BF16), st.astype(BF16))
    b_last = b[t_rows - 1:t_rows, :]
    kd = (kf * jnp.exp(b_last - b)).astype(BF16)
    st_ref[...] = jnp.exp(b_last) * st + _dot_tn(vb, kd)

    y = _rms(o, nw_ref[...]) * (zg * jax.nn.sigmoid(zg))
    o_ref[0] = y.astype(BF16)


def _hgrn(z, llb, l1m, omlb, nw, batch, seq):
    t = T_HGRN
    nc = seq // t
    n = batch * seq

    def zspec(g0):
        return pl.BlockSpec((1, t, LANE), lambda b, h, c: (g0 + h, b * nc + c, 0))

    def pspec():
        return pl.BlockSpec((1, LANE), lambda b, h, c: (0, h))

    return pl.pallas_call(
        _hgrn_kernel,
        grid=(batch, HEADS, nc),
        in_specs=[zspec(G_HQ), zspec(G_HF), zspec(G_HI), zspec(G_HG),
                  pspec(), pspec(), pspec(),
                  pl.BlockSpec((1, LANE), lambda b, h, c: (0, 0))],
        out_specs=pl.BlockSpec((1, t, LANE), lambda b, h, c: (h, b * nc + c, 0)),
        out_shape=jax.ShapeDtypeStruct((HEADS, n, LANE), BF16),
        scratch_shapes=[pltpu.VMEM((DH, DH), F32)],
        compiler_params=_cparams(("parallel", "parallel", "arbitrary")),
        name="hgrn2",
    )(z, z, z, z, llb, l1m, omlb, nw)


def _attn_kernel(lam_ref, q_ref, k_ref, v_ref, nw_ref, o_ref, sa_ref, sb_ref, m_ref, l_ref, acc_ref, *,
                 lam_init):
    qi = pl.program_id(2)
    tq = q_ref.shape[1]

    lp = lam_ref[...]
    lam = (jnp.exp(jnp.sum(lp[0:1] * lp[1:2], axis=-1, keepdims=True))
           - jnp.exp(jnp.sum(lp[2:3] * lp[3:4], axis=-1, keepdims=True)) + lam_init)

    qt = q_ref[0].astype(F32).T * (DQK ** -0.5 * LOG2E)
    sub = lax.broadcasted_iota(jnp.int32, (LANE, tq), 0)
    qts = jnp.concatenate([jnp.where(sub < DQK, qt, 0.0), jnp.where(sub >= DQK, qt, 0.0)],
                          axis=1).astype(BF16)

    m_ref[...] = jnp.full(m_ref.shape, NEG, F32)
    l_ref[...] = jnp.zeros(l_ref.shape, F32)
    acc_ref[...] = jnp.zeros(acc_ref.shape, F32)

    def scores(j, s_ref):
        off = pl.multiple_of(j * TK, TK)
        s_ref[...] = _dot(k_ref[0, pl.ds(off, TK), :], qts)

    def softmax_pv(j, s_ref, masked):
        off = pl.multiple_of(j * TK, TK)
        vt = v_ref[0, pl.ds(off, TK), :]
        s = s_ref[...]
        if masked:
            krow = lax.broadcasted_iota(jnp.int32, (TK, 2 * tq), 0)
            qcol = lax.broadcasted_iota(jnp.int32, (TK, 2 * tq), 1) & (tq - 1)
            s = jnp.where(krow <= qcol, s, NEG)
        m_prev = m_ref[...]
        m_cur = jnp.maximum(m_prev, jnp.max(s, axis=0, keepdims=True))
        alpha = jnp.exp2(m_prev - m_cur)
        p = jnp.exp2(s - m_cur)
        l_ref[...] = alpha * l_ref[...] + jnp.sum(p, axis=0, keepdims=True)
        acc_ref[...] = alpha * acc_ref[...] + _dot_tn(vt, p.astype(BF16))
        m_ref[...] = m_cur

    scores(0, sa_ref)

    def pair_body(t, carry):
        softmax_pv(2 * t, sa_ref, False)
        scores(2 * t + 1, sb_ref)
        softmax_pv(2 * t + 1, sb_ref, False)
        scores(2 * t + 2, sa_ref)
        return carry

    lax.fori_loop(0, qi // 2, pair_body, 0)

    @pl.when(qi % 2 == 0)
    def _():
        softmax_pv(qi, sa_ref, True)

    @pl.when(qi % 2 == 1)
    def _():
        softmax_pv(qi - 1, sa_ref, False)
        scores(qi, sb_ref)
        softmax_pv(qi, sb_ref, True)

    on = acc_ref[...] / l_ref[...]
    ot = on[:, :tq] - lam * on[:, tq:]
    yt = ot * lax.rsqrt(jnp.mean(ot * ot, axis=0, keepdims=True) + EPS) * nw_ref[...]
    o_ref[0] = (yt.T * (1.0 - lam_init)).astype(BF16)


def _attn(z, lam_params, nw, lam_init, batch, seq):
    assert TQ == TK
    nq = seq // TQ
    n = batch * seq
    return pl.pallas_call(
        functools.partial(_attn_kernel, lam_init=lam_init),
        grid=(batch, HEADS, nq),
        in_specs=[
            pl.BlockSpec((4, DQK), lambda b, h, i: (0, 0)),
            pl.BlockSpec((1, TQ, LANE), lambda b, h, i: (G_AQ + h, b * nq + i, 0)),
            pl.BlockSpec((1, seq, LANE), lambda b, h, i: (G_AK + h, b, 0)),
            pl.BlockSpec((1, seq, LANE), lambda b, h, i: (G_AV + h, b, 0)),
            pl.BlockSpec((DH, 1), lambda b, h, i: (0, 0)),
        ],
        out_specs=pl.BlockSpec((1, TQ, LANE), lambda b, h, i: (h, b * nq + i, 0)),
        out_shape=jax.ShapeDtypeStruct((HEADS, n, LANE), BF16),
        scratch_shapes=[pltpu.VMEM((TK, 2 * TQ), F32), pltpu.VMEM((TK, 2 * TQ), F32),
                        pltpu.VMEM((1, 2 * TQ), F32), pltpu.VMEM((1, 2 * TQ), F32),
                        pltpu.VMEM((DH, 2 * TQ), F32)],
        compiler_params=_cparams(("parallel", "parallel", "arbitrary")),
        name="diff_attn",
    )(lam_params, z, z, z, nw)


def _gmlp_kernel(u_ref, v_ref, lnw_ref, lnb_ref, ws_ref, bs_ref, o_ref):
    t_rows = u_ref.shape[1]
    ng = MIX // LANE
    vs = [_gelu(v_ref[g].astype(F32)) for g in range(ng)]
    mu = sum(jnp.sum(v, axis=-1, keepdims=True) for v in vs) * (1.0 / MIX)
    var = sum(jnp.sum(jnp.square(v - mu), axis=-1, keepdims=True) for v in vs) * (1.0 / MIX)
    rs = lax.rsqrt(var + EPS)
    row = lax.broadcasted_iota(jnp.int32, (GMLP_CHUNK, GMLP_CHUNK), 0)
    col = lax.broadcasted_iota(jnp.int32, (GMLP_CHUNK, GMLP_CHUNK), 1)
    for g in range(ng):
        vn = ((vs[g] - mu) * rs * lnw_ref[g] + lnb_ref[g]).astype(BF16)
        w = jnp.where(row >= col, ws_ref[g], 0.0).astype(BF16)
        for n in range(t_rows // GMLP_CHUNK):
            sl = slice(n * GMLP_CHUNK, (n + 1) * GMLP_CHUNK)
            mixed = _dot(w, vn[sl]) + bs_ref[g]
            o_ref[g, sl, :] = (_gelu(u_ref[g, sl, :].astype(F32)) * mixed).astype(BF16)


def _gmlp(z, lnw, lnb, ws, bs):
    n = z.shape[1]
    t = T_GMLP
    ng = MIX // LANE
    return pl.pallas_call(
        _gmlp_kernel,
        grid=(n // t,),
        in_specs=[
            pl.BlockSpec((ng, t, LANE), lambda i: (G_GU // ng, i, 0)),
            pl.BlockSpec((ng, t, LANE), lambda i: (G_GV // ng, i, 0)),
            pl.BlockSpec((ng, 1, LANE), lambda i: (0, 0, 0)),
            pl.BlockSpec((ng, 1, LANE), lambda i: (0, 0, 0)),
            pl.BlockSpec((ng, GMLP_CHUNK, GMLP_CHUNK), lambda i: (0, 0, 0)),
            pl.BlockSpec((ng, GMLP_CHUNK, 1), lambda i: (0, 0, 0)),
        ],
        out_specs=pl.BlockSpec((ng, t, LANE), lambda i: (0, i, 0)),
        out_shape=jax.ShapeDtypeStruct((ng, n, LANE), BF16),
        compiler_params=_cparams(("parallel",)),
        name="gmlp",
    )(z, z, lnw, lnb, ws, bs)


def _merge_kernel(x_ref, gl_ref, yh_ref, ya_ref, yg_ref, wh_ref, wa_ref, wg_ref, wo_ref, o_ref, m_ref):
    ng = MIX // LANE
    ys = [jnp.concatenate([r[g] for g in range(ng)], axis=-1) for r in (yh_ref, ya_ref, yg_ref)]
    ws = (wh_ref, wa_ref, wg_ref)
    cw = 2 * LANE
    gpb = D_MODEL // LANE
    for cc in range(D_MODEL // cw):
        acc = None
        for br in range(N_BRANCH):
            p = _dot(ys[br], ws[br][:, cc * cw:(cc + 1) * cw])
            g0 = br * gpb + 2 * cc
            gl = jnp.concatenate([gl_ref[g0], gl_ref[g0 + 1]], axis=-1).astype(F32)
            t = jax.nn.sigmoid(gl) * p
            acc = t if acc is None else acc + t
        m_ref[:, cc * cw:(cc + 1) * cw] = acc.astype(BF16)
    o_ref[...] = x_ref[...] + _dot(m_ref[...], wo_ref[...])


def _merge(x2, z, yh, ya, yg, wh, wa, wg, wo):
    n = x2.shape[0]
    tm = TM_PROJ
    ng = MIX // LANE
    ngate = N_BRANCH * D_MODEL // LANE

    def yspec():
        return pl.BlockSpec((ng, tm, LANE), lambda i: (0, i, 0))

    def wspec():
        return _resident((MIX, D_MODEL), lambda i: (0, 0))

    return pl.pallas_call(
        _merge_kernel,
        grid=(n // tm,),
        in_specs=[
            pl.BlockSpec((tm, D_MODEL), lambda i: (i, 0)),
            pl.BlockSpec((ngate, tm, LANE), lambda i: (G_GATE // ngate, i, 0)),
            yspec(), yspec(), yspec(),
            wspec(), wspec(), wspec(),
            _resident((D_MODEL, D_MODEL), lambda i: (0, 0)),
        ],
        out_specs=pl.BlockSpec((tm, D_MODEL), lambda i: (i, 0)),
        out_shape=jax.ShapeDtypeStruct((n, D_MODEL), F32),
        scratch_shapes=[pltpu.VMEM((tm, D_MODEL), BF16)],
        compiler_params=_cparams(("parallel",)),
        name="merge_out",
    )(x2, z, yh, ya, yg, wh, wa, wg, wo)


def _ffn_kernel(x_ref, nw_ref, w1_ref, w2_ref, fw_ref, o_ref, h_ref, ff_ref, *, final):
    x = x_ref[...]
    h_ref[...] = _rms(x, nw_ref[...]).astype(BF16)
    fc = D_MODEL
    for j in range(D_FF // fc):
        a = jnp.maximum(_dot(h_ref[...], w1_ref[:, j * fc:(j + 1) * fc]), 0.0)
        ff_ref[:, j * fc:(j + 1) * fc] = (a * a).astype(BF16)
    y = x + _dot(ff_ref[...], w2_ref[...])
    if final:
        y = _rms(y, fw_ref[...])
    o_ref[...] = y


def _ffn(x2, nw, w1, w2, fw, final):
    n = x2.shape[0]
    tm = TM_PROJ
    return pl.pallas_call(
        functools.partial(_ffn_kernel, final=final),
        grid=(n // tm,),
        in_specs=[
            pl.BlockSpec((tm, D_MODEL), lambda i: (i, 0)),
            _resident((1, D_MODEL), lambda i: (0, 0)),
            _resident((D_MODEL, D_FF), lambda i: (0, 0)),
            _resident((D_FF, D_MODEL), lambda i: (0, 0)),
            _resident((1, D_MODEL), lambda i: (0, 0)),
        ],
        out_specs=pl.BlockSpec((tm, D_MODEL), lambda i: (i, 0)),
        out_shape=jax.ShapeDtypeStruct((n, D_MODEL), F32),
        scratch_shapes=[pltpu.VMEM((tm, D_MODEL), BF16), pltpu.VMEM((tm, D_FF), BF16)],
        compiler_params=_cparams(("parallel",)),
        name="ffn",
    )(x2, nw, w1, w2, fw)


def kernel(x, norm_mix_w, w_in, hgrn_lb_logits, hgrn_norm_w, diff_lam_q1, diff_lam_k1, diff_lam_q2,
           diff_lam_k2, diff_norm_w, gmlp_ln_w, gmlp_ln_b, gmlp_w_s, gmlp_b_s, w_br_hgrn, w_br_attn,
           w_br_gmlp, w_out, norm_ff_w, w_ff1, w_ff2, final_norm_w):
    batch, seq, _ = x.shape
    depth = w_in.shape[0]
    n = batch * seq
    assert n % TM_PROJ == 0 and seq % T_HGRN == 0 and seq % TQ == 0 and n % T_GMLP == 0

    cum = jnp.cumsum(jax.nn.softmax(hgrn_lb_logits.astype(F32), axis=0), axis=0)
    lbs = cum - cum[0:1]
    log_lb = jnp.maximum(jnp.log(lbs), NEG)
    log_1m_lb = jnp.log1p(-lbs)
    one_m_lb = 1.0 - lbs

    split = 9 * MIX
    x2 = x.reshape(n, D_MODEL)
    for l in range(depth):
        w_l = jnp.concatenate([w_in[l][:, split:], w_in[l][:, :split]], axis=1).astype(BF16)
        z = _inproj(x2, norm_mix_w[l].reshape(1, D_MODEL), w_l)

        yh = _hgrn(z, log_lb[l].reshape(1, MIX), log_1m_lb[l].reshape(1, MIX),
                   one_m_lb[l].reshape(1, MIX), hgrn_norm_w[l].reshape(1, DH), batch, seq)
        lam_init = 0.8 - 0.6 * math.exp(-0.3 * l)
        lam_params = jnp.stack([diff_lam_q1[l], diff_lam_k1[l], diff_lam_q2[l], diff_lam_k2[l]]).astype(F32)
        ya = _attn(z, lam_params, diff_norm_w[l].reshape(DH, 1), lam_init, batch, seq)
        yg = _gmlp(z, gmlp_ln_w[l].reshape(MIX // LANE, 1, LANE), gmlp_ln_b[l].reshape(MIX // LANE, 1, LANE),
                   gmlp_w_s[l], gmlp_b_s[l][..., None])

        x2 = _merge(x2, z, yh, ya, yg, w_br_hgrn[l].astype(BF16), w_br_attn[l].astype(BF16),
                    w_br_gmlp[l].astype(BF16), w_out[l].astype(BF16))
        x2 = _ffn(x2, norm_ff_w[l].reshape(1, D_MODEL), w_ff1[l].astype(BF16), w_ff2[l].astype(BF16),
                  final_norm_w.reshape(1, D_MODEL), final=(l == depth - 1))
    return x2.reshape(batch, seq, D_MODEL)
```

```python
import functools
import math

import jax
import jax.numpy as jnp
from jax import lax
from jax.experimental import pallas as pl
from jax.experimental.pallas import tpu as pltpu

F32 = jnp.float32
BF16 = jnp.bfloat16

D_MODEL = 1024
MIX = 512
HEADS = 4
DH = 128
DQK = 64
D_FF = 4 * D_MODEL
N_BRANCH = 3
EPS = 1e-6
GMLP_CHUNK = 128
LANE = 128
IN_COLS = 9 * MIX + N_BRANCH * D_MODEL
NG = IN_COLS // LANE

G_GATE = 0
G_HQ, G_HF, G_HI, G_HG = 24, 28, 32, 36
G_AQ, G_AK, G_AV = 40, 44, 48
G_GU, G_GV = 52, 56

LOG2E = 1.4426950408889634
NEG = -1e30
VMEM_LIMIT = 56 * 1024 * 1024

TM_PROJ = 512
T_HGRN = 512
SUB = 16
TQ = 512
TK = 512
T_GMLP = 256
CW = 512


def _cparams(sem):
    return pltpu.CompilerParams(dimension_semantics=sem, vmem_limit_bytes=VMEM_LIMIT)


def _resident(shape, index_map):
    return pl.BlockSpec(shape, index_map, pipeline_mode=pl.Buffered(1))


def _rms(x, w):
    return x * lax.rsqrt(jnp.mean(x * x, axis=-1, keepdims=True) + EPS) * w


def _gelu(x):
    return 0.5 * x * (1.0 + lax.erf(x * (1.0 / math.sqrt(2.0))))


def _dot(a, b):
    return jnp.dot(a, b, preferred_element_type=F32)


def _dot_nt(a, b):
    return lax.dot_general(a, b, (((1,), (1,)), ((), ())), preferred_element_type=F32)


def _dot_tn(a, b):
    return lax.dot_general(a, b, (((0,), (0,)), ((), ())), preferred_element_type=F32)


def _inproj_kernel(x_ref, nw_ref, w_ref, z_ref, h_ref):
    h_ref[...] = _rms(x_ref[...], nw_ref[...]).astype(BF16)
    gpc = CW // LANE
    for jc in range(IN_COLS // CW):
        r = _dot(h_ref[...], w_ref[:, jc * CW:(jc + 1) * CW])
        for g in range(gpc):
            z_ref[jc * gpc + g] = r[:, g * LANE:(g + 1) * LANE].astype(BF16)


def _inproj(x2, nw, w):
    n = x2.shape[0]
    tm = TM_PROJ
    return pl.pallas_call(
        _inproj_kernel,
        grid=(n // tm,),
        in_specs=[
            pl.BlockSpec((tm, D_MODEL), lambda i: (i, 0)),
            _resident((1, D_MODEL), lambda i: (0, 0)),
            _resident((D_MODEL, IN_COLS), lambda i: (0, 0)),
        ],
        out_specs=pl.BlockSpec((NG, tm, LANE), lambda i: (0, i, 0)),
        out_shape=jax.ShapeDtypeStruct((NG, n, LANE), BF16),
        scratch_shapes=[pltpu.VMEM((tm, D_MODEL), BF16)],
        compiler_params=_cparams(("parallel",)),
        name="inproj",
    )(x2, nw, w)


def _sigmoid_parts(z):
    e = jnp.exp2(jnp.abs(z) * (-LOG2E))
    d = 1.0 + e
    r = 1.0 / d
    er = e * r
    pos = z >= 0.0
    return jnp.where(pos, r, er), jnp.where(pos, er, r), d


def _hgrn_kernel(q_ref, f_ref, i_ref, g_ref, llb_ref, l1m_ref, omlb_ref, nw_ref, o_ref, st_ref, tri_ref):
    t_rows = q_ref.shape[1]

    @pl.when(pl.program_id(2) == 0)
    def _():
        st_ref[...] = jnp.zeros_like(st_ref)
        row = lax.broadcasted_iota(jnp.int32, (t_rows, t_rows), 0)
        col = lax.broadcasted_iota(jnp.int32, (t_rows, t_rows), 1)
        tri_ref[...] = jnp.where(row >= col, 1.0, 0.0).astype(BF16)

    zq = q_ref[0].astype(F32)
    zf = f_ref[0].astype(F32)
    vi = i_ref[0].astype(F32)
    zg = g_ref[0].astype(F32)

    qf = zq * _sigmoid_parts(zq)[0]
    _, nsig, den = _sigmoid_parts(zf)
    c = l1m_ref[...] + (jnp.minimum(zf, 0.0) - jnp.log(den))
    a = llb_ref[...]
    logf = jnp.maximum(a, c) + jnp.log(1.0 + jnp.exp2(jnp.abs(a - c) * (-LOG2E)))
    kf = omlb_ref[...] * nsig
    lf2 = logf * LOG2E

    hi = lf2.astype(BF16)
    r1 = lf2 - hi.astype(F32)
    mid = r1.astype(BF16)
    lo = (r1 - mid.astype(F32)).astype(BF16)
    c3 = _dot(tri_ref[...], jnp.concatenate([hi, mid, lo], axis=1))
    b = c3[:, :DH] + c3[:, DH:2 * DH] + c3[:, 2 * DH:]

    vb = vi.astype(BF16)
    o = jnp.zeros((t_rows, DH), F32)

    h = SUB
    while h < t_rows:
        nb = t_rows // (2 * h)
        b3 = b.reshape(nb, 2 * h, DH)
        ref = b3[:, h:h + 1, :]
        qp = (qf.reshape(nb, 2 * h, DH)[:, h:, :] * jnp.exp2(b3[:, h:, :] - ref)).astype(BF16)
        kp = (kf.reshape(nb, 2 * h, DH)[:, :h, :] * jnp.exp2(ref - b3[:, :h, :])).astype(BF16)
        a_l = jnp.einsum('nqk,nsk->nqs', qp, kp, preferred_element_type=F32)
        o_l = jnp.einsum('nqs,nsv->nqv', a_l.astype(BF16), vb.reshape(nb, 2 * h, DH)[:, :h, :],
                         preferred_element_type=F32)
        o = o + jnp.concatenate([jnp.zeros_like(o_l), o_l], axis=1).reshape(t_rows, DH)
        h *= 2

    nb = t_rows // SUB
    hs = SUB // 2
    q3 = qf.reshape(nb, SUB, DH)
    b3 = b.reshape(nb, SUB, DH)
    c3s = (b - jnp.log2(kf)).reshape(nb, SUB, DH)
    t8 = lax.broadcasted_iota(jnp.int32, (nb, hs, DH), 1)
    lane = lax.broadcasted_iota(jnp.int32, (nb, hs, DH), 2)
    halves = []
    for half in range(2):
        rows = slice(half * hs, (half + 1) * hs)
        qh, bh = q3[:, rows, :], b3[:, rows, :]
        ah = jnp.zeros((nb, hs, DH), F32)
        for s in range((half + 1) * hs):
            d = bh - c3s[:, s:s + 1, :]
            if s >= half * hs:
                d = jnp.where(t8 >= s - half * hs, d, NEG)
            a_s = jnp.sum(qh * jnp.exp2(d), axis=-1, keepdims=True)
            ah = jnp.where(lane == s, a_s, ah)
        halves.append(ah)
    a_blk = jnp.concatenate(halves, axis=1)[:, :, :SUB].astype(BF16)
    o_blk = jnp.einsum('nts,nsv->ntv', a_blk, vb.reshape(nb, SUB, DH), preferred_element_type=F32)
    o = o + o_blk.reshape(t_rows, DH)

    st = st_ref[...]
    o = o + _dot_nt((qf * jnp.exp2(b)).astype(BF16), st.astype(BF16))
    b_last = b[t_rows - 1:t_rows, :]
    kd = (kf * jnp.exp2(b_last - b)).astype(BF16)
    st_ref[...] = jnp.exp2(b_last) * st + _dot_tn(vb, kd)

    y = _rms(o, nw_ref[...]) * (zg * _sigmoid_parts(zg)[0])
    o_ref[0] = y.astype(BF16)


def _hgrn(z, llb, l1m, omlb, nw, batch, seq):
    t = T_HGRN
    nc = seq // t
    n = batch * seq

    def zspec(g0):
        return pl.BlockSpec((1, t, LANE), lambda b, h, c: (g0 + h, b * nc + c, 0))

    def pspec():
        return pl.BlockSpec((1, LANE), lambda b, h, c: (0, h))

    return pl.pallas_call(
        _hgrn_kernel,
        grid=(batch, HEADS, nc),
        in_specs=[zspec(G_HQ), zspec(G_HF), zspec(G_HI), zspec(G_HG),
                  pspec(), pspec(), pspec(),
                  pl.BlockSpec((1, LANE), lambda b, h, c: (0, 0))],
        out_specs=pl.BlockSpec((1, t, LANE), lambda b, h, c: (h, b * nc + c, 0)),
        out_shape=jax.ShapeDtypeStruct((HEADS, n, LANE), BF16),
        scratch_shapes=[pltpu.VMEM((DH, DH), F32), pltpu.VMEM((t, t), BF16)],
        compiler_params=_cparams(("parallel", "parallel", "arbitrary")),
        name="hgrn2",
    )(z, z, z, z, llb, l1m, omlb, nw)


def _attn_kernel(lam_ref, q_ref, k_ref, v_ref, nw_ref, o_ref, sa_ref, sb_ref, m_ref, l_ref, acc_ref, *,
                 lam_init):
    qi = pl.program_id(2)
    tq = q_ref.shape[1]

    lp = lam_ref[...]
    lam = (jnp.exp(jnp.sum(lp[0:1] * lp[1:2], axis=-1, keepdims=True))
           - jnp.exp(jnp.sum(lp[2:3] * lp[3:4], axis=-1, keepdims=True)) + lam_init)

    qt = q_ref[0].astype(F32).T * (DQK ** -0.5 * LOG2E)
    sub = lax.broadcasted_iota(jnp.int32, (LANE, tq), 0)
    qts = jnp.concatenate([jnp.where(sub < DQK, qt, 0.0), jnp.where(sub >= DQK, qt, 0.0)],
                          axis=1).astype(BF16)

    m_ref[...] = jnp.full(m_ref.shape, NEG, F32)
    l_ref[...] = jnp.zeros(l_ref.shape, F32)
    acc_ref[...] = jnp.zeros(acc_ref.shape, F32)

    def scores(j, s_ref):
        off = pl.multiple_of(j * TK, TK)
        s_ref[...] = _dot(k_ref[0, pl.ds(off, TK), :], qts)

    def softmax_pv(j, s_ref, masked):
        off = pl.multiple_of(j * TK, TK)
        vt = v_ref[0, pl.ds(off, TK), :]
        s = s_ref[...]
        if masked:
            krow = lax.broadcasted_iota(jnp.int32, (TK, 2 * tq), 0)
            qcol = lax.broadcasted_iota(jnp.int32, (TK, 2 * tq), 1) & (tq - 1)
            s = jnp.where(krow <= qcol, s, NEG)
        m_prev = m_ref[...]
        m_cur = jnp.maximum(m_prev, jnp.max(s, axis=0, keepdims=True))
        alpha = jnp.exp2(m_prev - m_cur)
        p = jnp.exp2(s - m_cur)
        l_ref[...] = alpha * l_ref[...] + jnp.sum(p, axis=0, keepdims=True)
        acc_ref[...] = alpha * acc_ref[...] + _dot_tn(vt, p.astype(BF16))
        m_ref[...] = m_cur

    scores(0, sa_ref)

    def pair_body(t, carry):
        scores(2 * t + 1, sb_ref)
        softmax_pv(2 * t, sa_ref, False)
        scores(2 * t + 2, sa_ref)
        softmax_pv(2 * t + 1, sb_ref, False)
        return carry

    lax.fori_loop(0, qi // 2, pair_body, 0)

    @pl.when(qi % 2 == 0)
    def _():
        softmax_pv(qi, sa_ref, True)

    @pl.when(qi % 2 == 1)
    def _():
        scores(qi, sb_ref)
        softmax_pv(qi - 1, sa_ref, False)
        softmax_pv(qi, sb_ref, True)

    on = acc_ref[...] / l_ref[...]
    ot = on[:, :tq] - lam * on[:, tq:]
    yt = ot * lax.rsqrt(jnp.mean(ot * ot, axis=0, keepdims=True) + EPS) * nw_ref[...]
    o_ref[0] = (yt.T * (1.0 - lam_init)).astype(BF16)


def _attn(z, lam_params, nw, lam_init, batch, seq):
    assert TQ == TK
    nq = seq // TQ
    n = batch * seq
    return pl.pallas_call(
        functools.partial(_attn_kernel, lam_init=lam_init),
        grid=(batch, HEADS, nq),
        in_specs=[
            pl.BlockSpec((4, DQK), lambda b, h, i: (0, 0)),
            pl.BlockSpec((1, TQ, LANE), lambda b, h, i: (G_AQ + h, b * nq + i, 0)),
            pl.BlockSpec((1, seq, LANE), lambda b, h, i: (G_AK + h, b, 0)),
            pl.BlockSpec((1, seq, LANE), lambda b, h, i: (G_AV + h, b, 0)),
            pl.BlockSpec((DH, 1), lambda b, h, i: (0, 0)),
        ],
        out_specs=pl.BlockSpec((1, TQ, LANE), lambda b, h, i: (h, b * nq + i, 0)),
        out_shape=jax.ShapeDtypeStruct((HEADS, n, LANE), BF16),
        scratch_shapes=[pltpu.VMEM((TK, 2 * TQ), F32), pltpu.VMEM((TK, 2 * TQ), F32),
                        pltpu.VMEM((1, 2 * TQ), F32), pltpu.VMEM((1, 2 * TQ), F32),
                        pltpu.VMEM((DH, 2 * TQ), F32)],
        compiler_params=_cparams(("parallel", "parallel", "arbitrary")),
        name="diff_attn",
    )(lam_params, z, z, z, nw)


def _gmlp_kernel(u_ref, v_ref, lnw_ref, lnb_ref, ws_ref, bs_ref, o_ref):
    t_rows = u_ref.shape[1]
    ng = MIX // LANE
    vs = [_gelu(v_ref[g].astype(F32)) for g in range(ng)]
    mu = sum(jnp.sum(v, axis=-1, keepdims=True) for v in vs) * (1.0 / MIX)
    var = sum(jnp.sum(jnp.square(v - mu), axis=-1, keepdims=True) for v in vs) * (1.0 / MIX)
    rs = lax.rsqrt(var + EPS)
    row = lax.broadcasted_iota(jnp.int32, (GMLP_CHUNK, GMLP_CHUNK), 0)
    col = lax.broadcasted_iota(jnp.int32, (GMLP_CHUNK, GMLP_CHUNK), 1)
    for g in range(ng):
        vn = ((vs[g] - mu) * rs * lnw_ref[g] + lnb_ref[g]).astype(BF16)
        w = jnp.where(row >= col, ws_ref[g], 0.0).astype(BF16)
        for n in range(t_rows // GMLP_CHUNK):
            sl = slice(n * GMLP_CHUNK, (n + 1) * GMLP_CHUNK)
            mixed = _dot(w, vn[sl]) + bs_ref[g]
            o_ref[g, sl, :] = (_gelu(u_ref[g, sl, :].astype(F32)) * mixed).astype(BF16)


def _gmlp(z, lnw, lnb, ws, bs):
    n = z.shape[1]
    t = T_GMLP
    ng = MIX // LANE
    return pl.pallas_call(
        _gmlp_kernel,
        grid=(n // t,),
        in_specs=[
            pl.BlockSpec((ng, t, LANE), lambda i: (G_GU // ng, i, 0)),
            pl.BlockSpec((ng, t, LANE), lambda i: (G_GV // ng, i, 0)),
            pl.BlockSpec((ng, 1, LANE), lambda i: (0, 0, 0)),
            pl.BlockSpec((ng, 1, LANE), lambda i: (0, 0, 0)),
            pl.BlockSpec((ng, GMLP_CHUNK, GMLP_CHUNK), lambda i: (0, 0, 0)),
            pl.BlockSpec((ng, GMLP_CHUNK, 1), lambda i: (0, 0, 0)),
        ],
        out_specs=pl.BlockSpec((ng, t, LANE), lambda i: (0, i, 0)),
        out_shape=jax.ShapeDtypeStruct((ng, n, LANE), BF16),
        compiler_params=_cparams(("parallel",)),
        name="gmlp",
    )(z, z, lnw, lnb, ws, bs)


def _merge_kernel(x_ref, gl_ref, yh_ref, ya_ref, yg_ref, wh_ref, wa_ref, wg_ref, wo_ref, o_ref, m_ref):
    ng = MIX // LANE
    ys = [jnp.concatenate([r[g] for g in range(ng)], axis=-1) for r in (yh_ref, ya_ref, yg_ref)]
    ws = (wh_ref, wa_ref, wg_ref)
    cw = 2 * LANE
    gpb = D_MODEL // LANE
    for cc in range(D_MODEL // cw):
        acc = None
        for br in range(N_BRANCH):
            p = _dot(ys[br], ws[br][:, cc * cw:(cc + 1) * cw])
            g0 = br * gpb + 2 * cc
            gl = jnp.concatenate([gl_ref[g0], gl_ref[g0 + 1]], axis=-1).astype(F32)
            t = jax.nn.sigmoid(gl) * p
            acc = t if acc is None else acc + t
        m_ref[:, cc * cw:(cc + 1) * cw] = acc.astype(BF16)
    o_ref[...] = x_ref[...] + _dot(m_ref[...], wo_ref[...])


def _merge(x2, z, yh, ya, yg, wh, wa, wg, wo):
    n = x2.shape[0]
    tm = TM_PROJ
    ng = MIX // LANE
    ngate = N_BRANCH * D_MODEL // LANE

    def yspec():
        return pl.BlockSpec((ng, tm, LANE), lambda i: (0, i, 0))

    def wspec():
        return _resident((MIX, D_MODEL), lambda i: (0, 0))

    return pl.pallas_call(
        _merge_kernel,
        grid=(n // tm,),
        in_specs=[
            pl.BlockSpec((tm, D_MODEL), lambda i: (i, 0)),
            pl.BlockSpec((ngate, tm, LANE), lambda i: (G_GATE // ngate, i, 0)),
            yspec(), yspec(), yspec(),
            wspec(), wspec(), wspec(),
            _resident((D_MODEL, D_MODEL), lambda i: (0, 0)),
        ],
        out_specs=pl.BlockSpec((tm, D_MODEL), lambda i: (i, 0)),
        out_shape=jax.ShapeDtypeStruct((n, D_MODEL), F32),
        scratch_shapes=[pltpu.VMEM((tm, D_MODEL), BF16)],
        compiler_params=_cparams(("parallel",)),
        name="merge_out",
    )(x2, z, yh, ya, yg, wh, wa, wg, wo)


def _ffn_kernel(x_ref, nw_ref, w1_ref, w2_ref, fw_ref, o_ref, h_ref, ff_ref, *, final):
    x = x_ref[...]
    h_ref[...] = _rms(x, nw_ref[...]).astype(BF16)
    fc = D_MODEL
    for j in range(D_FF // fc):
        a = jnp.maximum(_dot(h_ref[...], w1_ref[:, j * fc:(j + 1) * fc]), 0.0)
        ff_ref[:, j * fc:(j + 1) * fc] = (a * a).astype(BF16)
    y = x + _dot(ff_ref[...], w2_ref[...])
    if final:
        y = _rms(y, fw_ref[...])
    o_ref[...] = y


def _ffn(x2, nw, w1, w2, fw, final):
    n = x2.shape[0]
    tm = TM_PROJ
    return pl.pallas_call(
        functools.partial(_ffn_kernel, final=final),
        grid=(n // tm,),
        in_specs=[
            pl.BlockSpec((tm, D_MODEL), lambda i: (i, 0)),
            _resident((1, D_MODEL), lambda i: (0, 0)),
            _resident((D_MODEL, D_FF), lambda i: (0, 0)),
            _resident((D_FF, D_MODEL), lambda i: (0, 0)),
            _resident((1, D_MODEL), lambda i: (0, 0)),
        ],
        out_specs=pl.BlockSpec((tm, D_MODEL), lambda i: (i, 0)),
        out_shape=jax.ShapeDtypeStruct((n, D_MODEL), F32),
        scratch_shapes=[pltpu.VMEM((tm, D_MODEL), BF16), pltpu.VMEM((tm, D_FF), BF16)],
        compiler_params=_cparams(("parallel",)),
        name="ffn",
    )(x2, nw, w1, w2, fw)


def kernel(x, norm_mix_w, w_in, hgrn_lb_logits, hgrn_norm_w, diff_lam_q1, diff_lam_k1, diff_lam_q2,
           diff_lam_k2, diff_norm_w, gmlp_ln_w, gmlp_ln_b, gmlp_w_s, gmlp_b_s, w_br_hgrn, w_br_attn,
           w_br_gmlp, w_out, norm_ff_w, w_ff1, w_ff2, final_norm_w):
    batch, seq, _ = x.shape
    depth = w_in.shape[0]
    n = batch * seq
    assert n % TM_PROJ == 0 and seq % T_HGRN == 0 and seq % TQ == 0 and n % T_GMLP == 0

    cum = jnp.cumsum(jax.nn.softmax(hgrn_lb_logits.astype(F32), axis=0), axis=0)
    lbs = cum - cum[0:1]
    log_lb = jnp.maximum(jnp.log(lbs), NEG)
    log_1m_lb = jnp.log1p(-lbs)
    one_m_lb = 1.0 - lbs

    split = 9 * MIX
    x2 = x.reshape(n, D_MODEL)
    for l in range(depth):
        w_l = jnp.concatenate([w_in[l][:, split:], w_in[l][:, :split]], axis=1).astype(BF16)
        z = _inproj(x2, norm_mix_w[l].reshape(1, D_MODEL), w_l)

        yh = _hgrn(z, log_lb[l].reshape(1, MIX), log_1m_lb[l].reshape(1, MIX),
                   one_m_lb[l].reshape(1, MIX), hgrn_norm_w[l].reshape(1, DH), batch, seq)
        lam_init = 0.8 - 0.6 * math.exp(-0.3 * l)
        lam_params = jnp.stack([diff_lam_q1[l], diff_lam_k1[l], diff_lam_q2[l], diff_lam_k2[l]]).astype(F32)
        ya = _attn(z, lam_params, diff_norm_w[l].reshape(DH, 1), lam_init, batch, seq)
        yg = _gmlp(z, gmlp_ln_w[l].reshape(MIX // LANE, 1, LANE), gmlp_ln_b[l].reshape(MIX // LANE, 1, LANE),
                   gmlp_w_s[l], gmlp_b_s[l][..., None])

        x2 = _merge(x2, z, yh, ya, yg, w_br_hgrn[l].astype(BF16), w_br_attn[l].astype(BF16),
                    w_br_gmlp[l].astype(BF16), w_out[l].astype(BF16))
        x2 = _ffn(x2, norm_ff_w[l].reshape(1, D_MODEL), w_ff1[l].astype(BF16), w_ff2[l].astype(BF16),
                  final_norm_w.reshape(1, D_MODEL), final=(l == depth - 1))
    return x2.reshape(batch, seq, D_MODEL)
```

```python
import functools
import math

import jax
import jax.numpy as jnp
from jax import lax
from jax.experimental import pallas as pl
from jax.experimental.pallas import tpu as pltpu

F32 = jnp.float32
BF16 = jnp.bfloat16

D_MODEL = 1024
MIX = 512
HEADS = 4
DH = 128
DQK = 64
D_FF = 4 * D_MODEL
N_BRANCH = 3
EPS = 1e-6
GMLP_CHUNK = 128
LANE = 128
IN_COLS = 9 * MIX + N_BRANCH * D_MODEL
NG = IN_COLS // LANE

G_GATE = 0
G_HQ, G_HF, G_HI, G_HG = 24, 28, 32, 36
G_AQ, G_AK, G_AV = 40, 44, 48
G_GU, G_GV = 52, 56

LOG2E = 1.4426950408889634
NEG = -1e30
VMEM_LIMIT = 56 * 1024 * 1024

TM_PROJ = 512
T_HGRN = 512
SUB = 16
TQ = 512
TK = 512
T_GMLP = 1024
CW = 512


def _cparams(sem):
    return pltpu.CompilerParams(dimension_semantics=sem, vmem_limit_bytes=VMEM_LIMIT)


def _resident(shape, index_map):
    return pl.BlockSpec(shape, index_map, pipeline_mode=pl.Buffered(1))


def _rms(x, w):
    return x * lax.rsqrt(jnp.mean(x * x, axis=-1, keepdims=True) + EPS) * w


def _gelu(x):
    return 0.5 * x * (1.0 + lax.erf(x * (1.0 / math.sqrt(2.0))))


def _dot(a, b):
    return jnp.dot(a, b, preferred_element_type=F32)


def _dot_nt(a, b):
    return lax.dot_general(a, b, (((1,), (1,)), ((), ())), preferred_element_type=F32)


def _dot_tn(a, b):
    return lax.dot_general(a, b, (((0,), (0,)), ((), ())), preferred_element_type=F32)


def _inproj_kernel(x_ref, nw_ref, w_ref, z_ref, h_ref):
    h_ref[...] = _rms(x_ref[...], nw_ref[...]).astype(BF16)
    gpc = CW // LANE
    for jc in range(IN_COLS // CW):
        r = _dot(h_ref[...], w_ref[:, jc * CW:(jc + 1) * CW])
        for g in range(gpc):
            z_ref[jc * gpc + g] = r[:, g * LANE:(g + 1) * LANE].astype(BF16)


def _inproj(x2, nw, w):
    n = x2.shape[0]
    tm = TM_PROJ
    return pl.pallas_call(
        _inproj_kernel,
        grid=(n // tm,),
        in_specs=[
            pl.BlockSpec((tm, D_MODEL), lambda i: (i, 0)),
            _resident((1, D_MODEL), lambda i: (0, 0)),
            _resident((D_MODEL, IN_COLS), lambda i: (0, 0)),
        ],
        out_specs=pl.BlockSpec((NG, tm, LANE), lambda i: (0, i, 0)),
        out_shape=jax.ShapeDtypeStruct((NG, n, LANE), BF16),
        scratch_shapes=[pltpu.VMEM((tm, D_MODEL), BF16)],
        compiler_params=_cparams(("parallel",)),
        name="inproj",
    )(x2, nw, w)


def _sigmoid(z):
    return 0.5 * jnp.tanh(0.5 * z) + 0.5


def _hgrn_kernel(q_ref, f_ref, i_ref, g_ref, llb_ref, l1m_ref, omlb_ref, nw_ref, o_ref, st_ref, tri_ref):
    t_rows = q_ref.shape[1]

    @pl.when(pl.program_id(2) == 0)
    def _():
        st_ref[...] = jnp.zeros_like(st_ref)
        row = lax.broadcasted_iota(jnp.int32, (t_rows, t_rows), 0)
        col = lax.broadcasted_iota(jnp.int32, (t_rows, t_rows), 1)
        tri_ref[...] = jnp.where(row >= col, 1.0, 0.0).astype(BF16)

    zq = q_ref[0].astype(F32)
    zf = f_ref[0].astype(F32)
    vi = i_ref[0].astype(F32)
    zg = g_ref[0].astype(F32)

    qf = zq * _sigmoid(zq)
    log_sig = jnp.minimum(zf, 0.0) - jnp.log(1.0 + jnp.exp2(jnp.abs(zf) * (-LOG2E)))
    c = l1m_ref[...] + log_sig
    a = llb_ref[...]
    logf = jnp.maximum(a, c) + jnp.log(1.0 + jnp.exp2(jnp.abs(a - c) * (-LOG2E)))
    kf = omlb_ref[...] * _sigmoid(-zf)
    lf2 = logf * LOG2E

    hi = lf2.astype(BF16)
    r1 = lf2 - hi.astype(F32)
    mid = r1.astype(BF16)
    lo = (r1 - mid.astype(F32)).astype(BF16)
    c3 = _dot(tri_ref[...], jnp.concatenate([hi, mid, lo], axis=1))
    b = c3[:, :DH] + c3[:, DH:2 * DH] + c3[:, 2 * DH:]

    vb = vi.astype(BF16)
    o = jnp.zeros((t_rows, DH), F32)

    h = SUB
    while h < t_rows:
        nb = t_rows // (2 * h)
        b3 = b.reshape(nb, 2 * h, DH)
        ref = b3[:, h:h + 1, :]
        qp = (qf.reshape(nb, 2 * h, DH)[:, h:, :] * jnp.exp2(b3[:, h:, :] - ref)).astype(BF16)
        kp = (kf.reshape(nb, 2 * h, DH)[:, :h, :] * jnp.exp2(ref - b3[:, :h, :])).astype(BF16)
        a_l = jnp.einsum('nqk,nsk->nqs', qp, kp, preferred_element_type=F32)
        o_l = jnp.einsum('nqs,nsv->nqv', a_l.astype(BF16), vb.reshape(nb, 2 * h, DH)[:, :h, :],
                         preferred_element_type=F32)
        o = o + jnp.concatenate([jnp.zeros_like(o_l), o_l], axis=1).reshape(t_rows, DH)
        h *= 2

    nb = t_rows // SUB
    hs = SUB // 2
    q3 = qf.reshape(nb, SUB, DH)
    b3 = b.reshape(nb, SUB, DH)
    c3s = (b - jnp.log2(kf)).reshape(nb, SUB, DH)
    t8 = lax.broadcasted_iota(jnp.int32, (nb, hs, DH), 1)
    lane = lax.broadcasted_iota(jnp.int32, (nb, hs, DH), 2)
    halves = []
    for half in range(2):
        rows = slice(half * hs, (half + 1) * hs)
        qh, bh = q3[:, rows, :], b3[:, rows, :]
        ah = jnp.zeros((nb, hs, DH), F32)
        for s in range((half + 1) * hs):
            d = bh - c3s[:, s:s + 1, :]
            if s >= half * hs:
                d = jnp.where(t8 >= s - half * hs, d, NEG)
            a_s = jnp.sum(qh * jnp.exp2(d), axis=-1, keepdims=True)
            ah = jnp.where(lane == s, a_s, ah)
        halves.append(ah)
    a_blk = jnp.concatenate(halves, axis=1)[:, :, :SUB].astype(BF16)
    o_blk = jnp.einsum('nts,nsv->ntv', a_blk, vb.reshape(nb, SUB, DH), preferred_element_type=F32)
    o = o + o_blk.reshape(t_rows, DH)

    st = st_ref[...]
    o = o + _dot_nt((qf * jnp.exp2(b)).astype(BF16), st.astype(BF16))
    b_last = b[t_rows - 1:t_rows, :]
    kd = (kf * jnp.exp2(b_last - b)).astype(BF16)
    st_ref[...] = jnp.exp2(b_last) * st + _dot_tn(vb, kd)

    y = _rms(o, nw_ref[...]) * (zg * _sigmoid(zg))
    o_ref[0] = y.astype(BF16)


def _hgrn(z, llb, l1m, omlb, nw, batch, seq):
    t = T_HGRN
    nc = seq // t
    n = batch * seq

    def zspec(g0):
        return pl.BlockSpec((1, t, LANE), lambda b, h, c: (g0 + h, b * nc + c, 0))

    def pspec():
        return pl.BlockSpec((1, LANE), lambda b, h, c: (0, h))

    return pl.pallas_call(
        _hgrn_kernel,
        grid=(batch, HEADS, nc),
        in_specs=[zspec(G_HQ), zspec(G_HF), zspec(G_HI), zspec(G_HG),
                  pspec(), pspec(), pspec(),
                  pl.BlockSpec((1, LANE), lambda b, h, c: (0, 0))],
        out_specs=pl.BlockSpec((1, t, LANE), lambda b, h, c: (h, b * nc + c, 0)),
        out_shape=jax.ShapeDtypeStruct((HEADS, n, LANE), BF16),
        scratch_shapes=[pltpu.VMEM((DH, DH), F32), pltpu.VMEM((t, t), BF16)],
        compiler_params=_cparams(("parallel", "parallel", "arbitrary")),
        name="hgrn2",
    )(z, z, z, z, llb, l1m, omlb, nw)


def _attn_kernel(lam_ref, q_ref, k_ref, v_ref, nw_ref, o_ref, sa_ref, sb_ref, m_ref, l_ref, acc_ref, *,
                 lam_init):
    qi = pl.program_id(2)
    tq = q_ref.shape[1]

    lp = lam_ref[...]
    lam = (jnp.exp(jnp.sum(lp[0:1] * lp[1:2], axis=-1, keepdims=True))
           - jnp.exp(jnp.sum(lp[2:3] * lp[3:4], axis=-1, keepdims=True)) + lam_init)

    qt = q_ref[0].astype(F32).T * (DQK ** -0.5 * LOG2E)
    sub = lax.broadcasted_iota(jnp.int32, (LANE, tq), 0)
    qts = jnp.concatenate([jnp.where(sub < DQK, qt, 0.0), jnp.where(sub >= DQK, qt, 0.0)],
                          axis=1).astype(BF16)

    m_ref[...] = jnp.full(m_ref.shape, NEG, F32)
    l_ref[...] = jnp.zeros(l_ref.shape, F32)
    acc_ref[...] = jnp.zeros(acc_ref.shape, F32)

    def scores(j, s_ref):
        off = pl.multiple_of(j * TK, TK)
        s_ref[...] = _dot(k_ref[0, pl.ds(off, TK), :], qts)

    def softmax_pv(j, s_ref, masked):
        off = pl.multiple_of(j * TK, TK)
        vt = v_ref[0, pl.ds(off, TK), :]
        s = s_ref[...]
        if masked:
            krow = lax.broadcasted_iota(jnp.int32, (TK, 2 * tq), 0)
            qcol = lax.broadcasted_iota(jnp.int32, (TK, 2 * tq), 1) & (tq - 1)
            s = jnp.where(krow <= qcol, s, NEG)
        m_prev = m_ref[...]
        m_cur = jnp.maximum(m_prev, jnp.max(s, axis=0, keepdims=True))
        alpha = jnp.exp2(m_prev - m_cur)
        p = jnp.exp2(s - m_cur)
        l_ref[...] = alpha * l_ref[...] + jnp.sum(p, axis=0, keepdims=True)
        acc_ref[...] = alpha * acc_ref[...] + _dot_tn(vt, p.astype(BF16))
        m_ref[...] = m_cur

    scores(0, sa_ref)

    def pair_body(t, carry):
        scores(2 * t + 1, sb_ref)
        softmax_pv(2 * t, sa_ref, False)
        scores(2 * t + 2, sa_ref)
        softmax_pv(2 * t + 1, sb_ref, False)
        return carry

    lax.fori_loop(0, qi // 2, pair_body, 0)

    @pl.when(qi % 2 == 0)
    def _():
        softmax_pv(qi, sa_ref, True)

    @pl.when(qi % 2 == 1)
    def _():
        scores(qi, sb_ref)
        softmax_pv(qi - 1, sa_ref, False)
        softmax_pv(qi, sb_ref, True)

    on = acc_ref[...] / l_ref[...]
    ot = on[:, :tq] - lam * on[:, tq:]
    yt = ot * lax.rsqrt(jnp.mean(ot * ot, axis=0, keepdims=True) + EPS) * nw_ref[...]
    o_ref[0] = (yt.T * (1.0 - lam_init)).astype(BF16)


def _attn(z, lam_params, nw, lam_init, batch, seq):
    assert TQ == TK
    nq = seq // TQ
    n = batch * seq
    return pl.pallas_call(
        functools.partial(_attn_kernel, lam_init=lam_init),
        grid=(batch, HEADS, nq),
        in_specs=[
            pl.BlockSpec((4, DQK), lambda b, h, i: (0, 0)),
            pl.BlockSpec((1, TQ, LANE), lambda b, h, i: (G_AQ + h, b * nq + i, 0)),
            pl.BlockSpec((1, seq, LANE), lambda b, h, i: (G_AK + h, b, 0)),
            pl.BlockSpec((1, seq, LANE), lambda b, h, i: (G_AV + h, b, 0)),
            pl.BlockSpec((DH, 1), lambda b, h, i: (0, 0)),
        ],
        out_specs=pl.BlockSpec((1, TQ, LANE), lambda b, h, i: (h, b * nq + i, 0)),
        out_shape=jax.ShapeDtypeStruct((HEADS, n, LANE), BF16),
        scratch_shapes=[pltpu.VMEM((TK, 2 * TQ), F32), pltpu.VMEM((TK, 2 * TQ), F32),
                        pltpu.VMEM((1, 2 * TQ), F32), pltpu.VMEM((1, 2 * TQ), F32),
                        pltpu.VMEM((DH, 2 * TQ), F32)],
        compiler_params=_cparams(("parallel", "parallel", "arbitrary")),
        name="diff_attn",
    )(lam_params, z, z, z, nw)


def _gmlp_kernel(u_ref, v_ref, lnw_ref, lnb_ref, ws_ref, bs_ref, o_ref):
    t_rows = u_ref.shape[1]
    ng = MIX // LANE
    vs = [_gelu(v_ref[g].astype(F32)) for g in range(ng)]
    mu = sum(jnp.sum(v, axis=-1, keepdims=True) for v in vs) * (1.0 / MIX)
    var = sum(jnp.sum(jnp.square(v - mu), axis=-1, keepdims=True) for v in vs) * (1.0 / MIX)
    rs = lax.rsqrt(var + EPS)
    row = lax.broadcasted_iota(jnp.int32, (GMLP_CHUNK, GMLP_CHUNK), 0)
    col = lax.broadcasted_iota(jnp.int32, (GMLP_CHUNK, GMLP_CHUNK), 1)
    for g in range(ng):
        vn = ((vs[g] - mu) * rs * lnw_ref[g] + lnb_ref[g]).astype(BF16)
        w = jnp.where(row >= col, ws_ref[g], 0.0).astype(BF16)
        for n in range(t_rows // GMLP_CHUNK):
            sl = slice(n * GMLP_CHUNK, (n + 1) * GMLP_CHUNK)
            mixed = _dot(w, vn[sl]) + bs_ref[g]
            o_ref[g, sl, :] = (_gelu(u_ref[g, sl, :].astype(F32)) * mixed).astype(BF16)


def _gmlp(z, lnw, lnb, ws, bs):
    n = z.shape[1]
    t = T_GMLP
    ng = MIX // LANE
    return pl.pallas_call(
        _gmlp_kernel,
        grid=(n // t,),
        in_specs=[
            pl.BlockSpec((ng, t, LANE), lambda i: (G_GU // ng, i, 0)),
            pl.BlockSpec((ng, t, LANE), lambda i: (G_GV // ng, i, 0)),
            pl.BlockSpec((ng, 1, LANE), lambda i: (0, 0, 0)),
            pl.BlockSpec((ng, 1, LANE), lambda i: (0, 0, 0)),
            pl.BlockSpec((ng, GMLP_CHUNK, GMLP_CHUNK), lambda i: (0, 0, 0)),
            pl.BlockSpec((ng, GMLP_CHUNK, 1), lambda i: (0, 0, 0)),
        ],
        out_specs=pl.BlockSpec((ng, t, LANE), lambda i: (0, i, 0)),
        out_shape=jax.ShapeDtypeStruct((ng, n, LANE), BF16),
        compiler_params=_cparams(("parallel",)),
        name="gmlp",
    )(z, z, lnw, lnb, ws, bs)


def _merge_kernel(x_ref, gl_ref, yh_ref, ya_ref, yg_ref, wh_ref, wa_ref, wg_ref, wo_ref, o_ref, m_ref):
    ng = MIX // LANE
    ys = [jnp.concatenate([r[g] for g in range(ng)], axis=-1) for r in (yh_ref, ya_ref, yg_ref)]
    ws = (wh_ref, wa_ref, wg_ref)
    cw = 2 * LANE
    gpb = D_MODEL // LANE
    for cc in range(D_MODEL // cw):
        acc = None
        for br in range(N_BRANCH):
            p = _dot(ys[br], ws[br][:, cc * cw:(cc + 1) * cw])
            g0 = br * gpb + 2 * cc
            gl = jnp.concatenate([gl_ref[g0], gl_ref[g0 + 1]], axis=-1).astype(F32)
            t = _sigmoid(gl) * p
            acc = t if acc is None else acc + t
        m_ref[:, cc * cw:(cc + 1) * cw] = acc.astype(BF16)
    o_ref[...] = x_ref[...] + _dot(m_ref[...], wo_ref[...])


def _merge(x2, z, yh, ya, yg, wh, wa, wg, wo):
    n = x2.shape[0]
    tm = TM_PROJ
    ng = MIX // LANE
    ngate = N_BRANCH * D_MODEL // LANE

    def yspec():
        return pl.BlockSpec((ng, tm, LANE), lambda i: (0, i, 0))

    def wspec():
        return _resident((MIX, D_MODEL), lambda i: (0, 0))

    return pl.pallas_call(
        _merge_kernel,
        grid=(n // tm,),
        in_specs=[
            pl.BlockSpec((tm, D_MODEL), lambda i: (i, 0)),
            pl.BlockSpec((ngate, tm, LANE), lambda i: (G_GATE // ngate, i, 0)),
            yspec(), yspec(), yspec(),
            wspec(), wspec(), wspec(),
            _resident((D_MODEL, D_MODEL), lambda i: (0, 0)),
        ],
        out_specs=pl.BlockSpec((tm, D_MODEL), lambda i: (i, 0)),
        out_shape=jax.ShapeDtypeStruct((n, D_MODEL), F32),
        scratch_shapes=[pltpu.VMEM((tm, D_MODEL), BF16)],
        compiler_params=_cparams(("parallel",)),
        name="merge_out",
    )(x2, z, yh, ya, yg, wh, wa, wg, wo)


def _ffn_kernel(x_ref, nw_ref, w1_ref, w2_ref, fw_ref, o_ref, h_ref, ff_ref, *, final):
    x = x_ref[...]
    h_ref[...] = _rms(x, nw_ref[...]).astype(BF16)
    fc = D_MODEL
    for j in range(D_FF // fc):
        a = jnp.maximum(_dot(h_ref[...], w1_ref[:, j * fc:(j + 1) * fc]), 0.0)
        ff_ref[:, j * fc:(j + 1) * fc] = (a * a).astype(BF16)
    y = x + _dot(ff_ref[...], w2_ref[...])
    if final:
        y = _rms(y, fw_ref[...])
    o_ref[...] = y


def _ffn(x2, nw, w1, w2, fw, final):
    n = x2.shape[0]
    tm = TM_PROJ
    return pl.pallas_call(
        functools.partial(_ffn_kernel, final=final),
        grid=(n // tm,),
        in_specs=[
            pl.BlockSpec((tm, D_MODEL), lambda i: (i, 0)),
            _resident((1, D_MODEL), lambda i: (0, 0)),
            _resident((D_MODEL, D_FF), lambda i: (0, 0)),
            _resident((D_FF, D_MODEL), lambda i: (0, 0)),
            _resident((1, D_MODEL), lambda i: (0, 0)),
        ],
        out_specs=pl.BlockSpec((tm, D_MODEL), lambda i: (i, 0)),
        out_shape=jax.ShapeDtypeStruct((n, D_MODEL), F32),
        scratch_shapes=[pltpu.VMEM((tm, D_MODEL), BF16), pltpu.VMEM((tm, D_FF), BF16)],
        compiler_params=_cparams(("parallel",)),
        name="ffn",
    )(x2, nw, w1, w2, fw)


def kernel(x, norm_mix_w, w_in, hgrn_lb_logits, hgrn_norm_w, diff_lam_q1, diff_lam_k1, diff_lam_q2,
           diff_lam_k2, diff_norm_w, gmlp_ln_w, gmlp_ln_b, gmlp_w_s, gmlp_b_s, w_br_hgrn, w_br_attn,
           w_br_gmlp, w_out, norm_ff_w, w_ff1, w_ff2, final_norm_w):
    batch, seq, _ = x.shape
    depth = w_in.shape[0]
    n = batch * seq
    assert n % TM_PROJ == 0 and seq % T_HGRN == 0 and seq % TQ == 0 and n % T_GMLP == 0

    cum = jnp.cumsum(jax.nn.softmax(hgrn_lb_logits.astype(F32), axis=0), axis=0)
    lbs = cum - cum[0:1]
    log_lb = jnp.maximum(jnp.log(lbs), NEG)
    log_1m_lb = jnp.log1p(-lbs)
    one_m_lb = 1.0 - lbs

    split = 9 * MIX
    x2 = x.reshape(n, D_MODEL)
    for l in range(depth):
        w_l = jnp.concatenate([w_in[l][:, split:], w_in[l][:, :split]], axis=1).astype(BF16)
        z = _inproj(x2, norm_mix_w[l].reshape(1, D_MODEL), w_l)

        yh = _hgrn(z, log_lb[l].reshape(1, MIX), log_1m_lb[l].reshape(1, MIX),
                   one_m_lb[l].reshape(1, MIX), hgrn_norm_w[l].reshape(1, DH), batch, seq)
        lam_init = 0.8 - 0.6 * math.exp(-0.3 * l)
        lam_params = jnp.stack([diff_lam_q1[l], diff_lam_k1[l], diff_lam_q2[l], diff_lam_k2[l]]).astype(F32)
        ya = _attn(z, lam_params, diff_norm_w[l].reshape(DH, 1), lam_init, batch, seq)
        yg = _gmlp(z, gmlp_ln_w[l].reshape(MIX // LANE, 1, LANE), gmlp_ln_b[l].reshape(MIX // LANE, 1, LANE),
                   gmlp_w_s[l], gmlp_b_s[l][..., None])

        x2 = _merge(x2, z, yh, ya, yg, w_br_hgrn[l].astype(BF16), w_br_attn[l].astype(BF16),
                    w_br_gmlp[l].astype(BF16), w_out[l].astype(BF16))
        x2 = _ffn(x2, norm_ff_w[l].reshape(1, D_MODEL), w_ff1[l].astype(BF16), w_ff2[l].astype(BF16),
                  final_norm_w.reshape(1, D_MODEL), final=(l == depth - 1))
    return x2.reshape(batch, seq, D_MODEL)
```

```python
import functools
import math

import jax
import jax.numpy as jnp
from jax import lax
from jax.experimental import pallas as pl
from jax.experimental.pallas import tpu as pltpu

F32 = jnp.float32
BF16 = jnp.bfloat16

D_MODEL = 1024
MIX = 512
HEADS = 4
DH = 128
DQK = 64
D_FF = 4 * D_MODEL
N_BRANCH = 3
EPS = 1e-6
GMLP_CHUNK = 128
LANE = 128
IN_COLS = 9 * MIX + N_BRANCH * D_MODEL
NG = IN_COLS // LANE

G_GATE = 0
G_HQ, G_HF, G_HI, G_HG = 24, 28, 32, 36
G_AQ, G_AK, G_AV = 40, 44, 48
G_GU, G_GV = 52, 56

LOG2E = 1.4426950408889634
NEG = -1e30
VMEM_LIMIT = 56 * 1024 * 1024

TM_PROJ = 512
T_HGRN = 512
HGRN_HEADS_PER_STEP = 2
SUB = 16
TQ = 512
TK = 512
T_GMLP = 1024
CW = 512


def _cparams(sem):
    return pltpu.CompilerParams(dimension_semantics=sem, vmem_limit_bytes=VMEM_LIMIT)


def _resident(shape, index_map):
    return pl.BlockSpec(shape, index_map, pipeline_mode=pl.Buffered(1))


def _rms(x, w):
    return x * lax.rsqrt(jnp.mean(x * x, axis=-1, keepdims=True) + EPS) * w


def _gelu(x):
    return 0.5 * x * (1.0 + lax.erf(x * (1.0 / math.sqrt(2.0))))


def _dot(a, b):
    return jnp.dot(a, b, preferred_element_type=F32)


def _dot_nt(a, b):
    return lax.dot_general(a, b, (((1,), (1,)), ((), ())), preferred_element_type=F32)


def _dot_tn(a, b):
    return lax.dot_general(a, b, (((0,), (0,)), ((), ())), preferred_element_type=F32)


def _inproj_kernel(x_ref, nw_ref, w_ref, z_ref, h_ref):
    h_ref[...] = _rms(x_ref[...], nw_ref[...]).astype(BF16)
    gpc = CW // LANE
    for jc in range(IN_COLS // CW):
        r = _dot(h_ref[...], w_ref[:, jc * CW:(jc + 1) * CW])
        for g in range(gpc):
            z_ref[jc * gpc + g] = r[:, g * LANE:(g + 1) * LANE].astype(BF16)


def _inproj(x2, nw, w):
    n = x2.shape[0]
    tm = TM_PROJ
    return pl.pallas_call(
        _inproj_kernel,
        grid=(n // tm,),
        in_specs=[
            pl.BlockSpec((tm, D_MODEL), lambda i: (i, 0)),
            _resident((1, D_MODEL), lambda i: (0, 0)),
            _resident((D_MODEL, IN_COLS), lambda i: (0, 0)),
        ],
        out_specs=pl.BlockSpec((NG, tm, LANE), lambda i: (0, i, 0)),
        out_shape=jax.ShapeDtypeStruct((NG, n, LANE), BF16),
        scratch_shapes=[pltpu.VMEM((tm, D_MODEL), BF16)],
        compiler_params=_cparams(("parallel",)),
        name="inproj",
    )(x2, nw, w)


def _sigmoid(z):
    return 0.5 * jnp.tanh(0.5 * z) + 0.5


def _hgrn_kernel(q_ref, f_ref, i_ref, g_ref, llb_ref, l1m_ref, omlb_ref, nw_ref, o_ref, st_ref, tri_ref):
    t_rows = q_ref.shape[1]

    @pl.when(pl.program_id(2) == 0)
    def _():
        st_ref[...] = jnp.zeros_like(st_ref)
        row = lax.broadcasted_iota(jnp.int32, (t_rows, t_rows), 0)
        col = lax.broadcasted_iota(jnp.int32, (t_rows, t_rows), 1)
        tri_ref[...] = jnp.where(row >= col, 1.0, 0.0).astype(BF16)

    for hd in range(q_ref.shape[0]):
        _hgrn_head(hd, q_ref, f_ref, i_ref, g_ref, llb_ref, l1m_ref, omlb_ref, nw_ref, o_ref, st_ref, tri_ref)


def _hgrn_head(hd, q_ref, f_ref, i_ref, g_ref, llb_ref, l1m_ref, omlb_ref, nw_ref, o_ref, st_ref, tri_ref):
    t_rows = q_ref.shape[1]
    chan = slice(hd * DH, (hd + 1) * DH)

    zq = q_ref[hd].astype(F32)
    zf = f_ref[hd].astype(F32)
    vi = i_ref[hd].astype(F32)
    zg = g_ref[hd].astype(F32)

    qf = zq * _sigmoid(zq)
    log_sig = jnp.minimum(zf, 0.0) - jnp.log(1.0 + jnp.exp2(jnp.abs(zf) * (-LOG2E)))
    c = l1m_ref[:, chan] + log_sig
    a = llb_ref[:, chan]
    logf = jnp.maximum(a, c) + jnp.log(1.0 + jnp.exp2(jnp.abs(a - c) * (-LOG2E)))
    kf = omlb_ref[:, chan] * _sigmoid(-zf)
    lf2 = logf * LOG2E

    hi = lf2.astype(BF16)
    r1 = lf2 - hi.astype(F32)
    mid = r1.astype(BF16)
    lo = (r1 - mid.astype(F32)).astype(BF16)
    c3 = _dot(tri_ref[...], jnp.concatenate([hi, mid, lo], axis=1))
    b = c3[:, :DH] + c3[:, DH:2 * DH] + c3[:, 2 * DH:]

    vb = vi.astype(BF16)
    o = jnp.zeros((t_rows, DH), F32)

    h = SUB
    while h < t_rows:
        nb = t_rows // (2 * h)
        b3 = b.reshape(nb, 2 * h, DH)
        ref = b3[:, h:h + 1, :]
        qp = (qf.reshape(nb, 2 * h, DH)[:, h:, :] * jnp.exp2(b3[:, h:, :] - ref)).astype(BF16)
        kp = (kf.reshape(nb, 2 * h, DH)[:, :h, :] * jnp.exp2(ref - b3[:, :h, :])).astype(BF16)
        a_l = jnp.einsum('nqk,nsk->nqs', qp, kp, preferred_element_type=F32)
        o_l = jnp.einsum('nqs,nsv->nqv', a_l.astype(BF16), vb.reshape(nb, 2 * h, DH)[:, :h, :],
                         preferred_element_type=F32)
        o = o + jnp.concatenate([jnp.zeros_like(o_l), o_l], axis=1).reshape(t_rows, DH)
        h *= 2

    nb = t_rows // SUB
    hs = SUB // 2
    q3 = qf.reshape(nb, SUB, DH)
    b3 = b.reshape(nb, SUB, DH)
    c3s = (b - jnp.log2(kf)).reshape(nb, SUB, DH)
    t8 = lax.broadcasted_iota(jnp.int32, (nb, hs, DH), 1)
    lane = lax.broadcasted_iota(jnp.int32, (nb, hs, DH), 2)
    halves = []
    for half in range(2):
        rows = slice(half * hs, (half + 1) * hs)
        qh, bh = q3[:, rows, :], b3[:, rows, :]
        ah = jnp.zeros((nb, hs, DH), F32)
        for s in range((half + 1) * hs):
            d = bh - c3s[:, s:s + 1, :]
            if s >= half * hs:
                d = jnp.where(t8 >= s - half * hs, d, NEG)
            a_s = jnp.sum(qh * jnp.exp2(d), axis=-1, keepdims=True)
            ah = jnp.where(lane == s, a_s, ah)
        halves.append(ah)
    a_blk = jnp.concatenate(halves, axis=1)[:, :, :SUB].astype(BF16)
    o_blk = jnp.einsum('nts,nsv->ntv', a_blk, vb.reshape(nb, SUB, DH), preferred_element_type=F32)
    o = o + o_blk.reshape(t_rows, DH)

    st = st_ref[hd]
    o = o + _dot_nt((qf * jnp.exp2(b)).astype(BF16), st.astype(BF16))
    b_last = b[t_rows - 1:t_rows, :]
    kd = (kf * jnp.exp2(b_last - b)).astype(BF16)
    st_ref[hd] = jnp.exp2(b_last) * st + _dot_tn(vb, kd)

    y = _rms(o, nw_ref[...]) * (zg * _sigmoid(zg))
    o_ref[hd] = y.astype(BF16)


def _hgrn(z, llb, l1m, omlb, nw, batch, seq):
    t = T_HGRN
    nc = seq // t
    n = batch * seq

    hps = HGRN_HEADS_PER_STEP

    def zspec(g0):
        return pl.BlockSpec((hps, t, LANE), lambda b, h, c: (g0 // hps + h, b * nc + c, 0))

    def pspec():
        return pl.BlockSpec((1, hps * LANE), lambda b, h, c: (0, h))

    return pl.pallas_call(
        _hgrn_kernel,
        grid=(batch, HEADS // hps, nc),
        in_specs=[zspec(G_HQ), zspec(G_HF), zspec(G_HI), zspec(G_HG),
                  pspec(), pspec(), pspec(),
                  pl.BlockSpec((1, LANE), lambda b, h, c: (0, 0))],
        out_specs=pl.BlockSpec((hps, t, LANE), lambda b, h, c: (h, b * nc + c, 0)),
        out_shape=jax.ShapeDtypeStruct((HEADS, n, LANE), BF16),
        scratch_shapes=[pltpu.VMEM((hps, DH, DH), F32), pltpu.VMEM((t, t), BF16)],
        compiler_params=_cparams(("parallel", "parallel", "arbitrary")),
        name="hgrn2",
    )(z, z, z, z, llb, l1m, omlb, nw)


def _attn_kernel(lam_ref, q_ref, k_ref, v_ref, nw_ref, o_ref, sa_ref, sb_ref, m_ref, l_ref, acc_ref, *,
                 lam_init):
    qi = pl.program_id(2)
    tq = q_ref.shape[1]

    lp = lam_ref[...]
    lam = (jnp.exp(jnp.sum(lp[0:1] * lp[1:2], axis=-1, keepdims=True))
           - jnp.exp(jnp.sum(lp[2:3] * lp[3:4], axis=-1, keepdims=True)) + lam_init)

    qt = q_ref[0].astype(F32).T * (DQK ** -0.5 * LOG2E)
    sub = lax.broadcasted_iota(jnp.int32, (LANE, tq), 0)
    qts = jnp.concatenate([jnp.where(sub < DQK, qt, 0.0), jnp.where(sub >= DQK, qt, 0.0)],
                          axis=1).astype(BF16)

    m_ref[...] = jnp.full(m_ref.shape, NEG, F32)
    l_ref[...] = jnp.zeros(l_ref.shape, F32)
    acc_ref[...] = jnp.zeros(acc_ref.shape, F32)

    def scores(j, s_ref):
        off = pl.multiple_of(j * TK, TK)
        s = _dot(k_ref[0, pl.ds(off, TK), :], qts)
        s_ref[0:TK, :] = s
        s_ref[TK:TK + 1, :] = jnp.max(s, axis=0, keepdims=True)

    def softmax_pv(j, s_ref, masked):
        off = pl.multiple_of(j * TK, TK)
        vt = v_ref[0, pl.ds(off, TK), :]
        s = s_ref[0:TK, :]
        if masked:
            krow = lax.broadcasted_iota(jnp.int32, (TK, 2 * tq), 0)
            qcol = lax.broadcasted_iota(jnp.int32, (TK, 2 * tq), 1) & (tq - 1)
            s = jnp.where(krow <= qcol, s, NEG)
            tile_max = jnp.max(s, axis=0, keepdims=True)
        else:
            tile_max = s_ref[TK:TK + 1, :]
        m_prev = m_ref[...]
        m_cur = jnp.maximum(m_prev, tile_max)
        alpha = jnp.exp2(m_prev - m_cur)
        p = jnp.exp2(s - m_cur)
        l_ref[...] = alpha * l_ref[...] + jnp.sum(p, axis=0, keepdims=True)
        acc_ref[...] = alpha * acc_ref[...] + _dot_tn(vt, p.astype(BF16))
        m_ref[...] = m_cur

    scores(0, sa_ref)

    def pair_body(t, carry):
        scores(2 * t + 1, sb_ref)
        softmax_pv(2 * t, sa_ref, False)
        scores(2 * t + 2, sa_ref)
        softmax_pv(2 * t + 1, sb_ref, False)
        return carry

    lax.fori_loop(0, qi // 2, pair_body, 0)

    @pl.when(qi % 2 == 0)
    def _():
        softmax_pv(qi, sa_ref, True)

    @pl.when(qi % 2 == 1)
    def _():
        scores(qi, sb_ref)
        softmax_pv(qi - 1, sa_ref, False)
        softmax_pv(qi, sb_ref, True)

    on = acc_ref[...] / l_ref[...]
    ot = on[:, :tq] - lam * on[:, tq:]
    yt = ot * lax.rsqrt(jnp.mean(ot * ot, axis=0, keepdims=True) + EPS) * nw_ref[...]
    o_ref[0] = (yt.T * (1.0 - lam_init)).astype(BF16)


def _attn(z, lam_params, nw, lam_init, batch, seq):
    assert TQ == TK
    nq = seq // TQ
    n = batch * seq
    return pl.pallas_call(
        functools.partial(_attn_kernel, lam_init=lam_init),
        grid=(batch, HEADS, nq),
        in_specs=[
            pl.BlockSpec((4, DQK), lambda b, h, i: (0, 0)),
            pl.BlockSpec((1, TQ, LANE), lambda b, h, i: (G_AQ + h, b * nq + i, 0)),
            pl.BlockSpec((1, seq, LANE), lambda b, h, i: (G_AK + h, b, 0)),
            pl.BlockSpec((1, seq, LANE), lambda b, h, i: (G_AV + h, b, 0)),
            pl.BlockSpec((DH, 1), lambda b, h, i: (0, 0)),
        ],
        out_specs=pl.BlockSpec((1, TQ, LANE), lambda b, h, i: (h, b * nq + i, 0)),
        out_shape=jax.ShapeDtypeStruct((HEADS, n, LANE), BF16),
        scratch_shapes=[pltpu.VMEM((TK + 8, 2 * TQ), F32), pltpu.VMEM((TK + 8, 2 * TQ), F32),
                        pltpu.VMEM((1, 2 * TQ), F32), pltpu.VMEM((1, 2 * TQ), F32),
                        pltpu.VMEM((DH, 2 * TQ), F32)],
        compiler_params=_cparams(("parallel", "parallel", "arbitrary")),
        name="diff_attn",
    )(lam_params, z, z, z, nw)


def _gmlp_kernel(u_ref, v_ref, lnw_ref, lnb_ref, ws_ref, bs_ref, o_ref):
    t_rows = u_ref.shape[1]
    ng = MIX // LANE
    vs = [_gelu(v_ref[g].astype(F32)) for g in range(ng)]
    mu = sum(jnp.sum(v, axis=-1, keepdims=True) for v in vs) * (1.0 / MIX)
    var = sum(jnp.sum(jnp.square(v - mu), axis=-1, keepdims=True) for v in vs) * (1.0 / MIX)
    rs = lax.rsqrt(var + EPS)
    row = lax.broadcasted_iota(jnp.int32, (GMLP_CHUNK, GMLP_CHUNK), 0)
    col = lax.broadcasted_iota(jnp.int32, (GMLP_CHUNK, GMLP_CHUNK), 1)
    for g in range(ng):
        vn = ((vs[g] - mu) * rs * lnw_ref[g] + lnb_ref[g]).astype(BF16)
        w = jnp.where(row >= col, ws_ref[g], 0.0).astype(BF16)
        for n in range(t_rows // GMLP_CHUNK):
            sl = slice(n * GMLP_CHUNK, (n + 1) * GMLP_CHUNK)
            mixed = _dot(w, vn[sl]) + bs_ref[g]
            o_ref[g, sl, :] = (_gelu(u_ref[g, sl, :].astype(F32)) * mixed).astype(BF16)


def _gmlp(z, lnw, lnb, ws, bs):
    n = z.shape[1]
    t = T_GMLP
    ng = MIX // LANE
    return pl.pallas_call(
        _gmlp_kernel,
        grid=(n // t,),
        in_specs=[
            pl.BlockSpec((ng, t, LANE), lambda i: (G_GU // ng, i, 0)),
            pl.BlockSpec((ng, t, LANE), lambda i: (G_GV // ng, i, 0)),
            pl.BlockSpec((ng, 1, LANE), lambda i: (0, 0, 0)),
            pl.BlockSpec((ng, 1, LANE), lambda i: (0, 0, 0)),
            pl.BlockSpec((ng, GMLP_CHUNK, GMLP_CHUNK), lambda i: (0, 0, 0)),
            pl.BlockSpec((ng, GMLP_CHUNK, 1), lambda i: (0, 0, 0)),
        ],
        out_specs=pl.BlockSpec((ng, t, LANE), lambda i: (0, i, 0)),
        out_shape=jax.ShapeDtypeStruct((ng, n, LANE), BF16),
        compiler_params=_cparams(("parallel",)),
        name="gmlp",
    )(z, z, lnw, lnb, ws, bs)


def _merge_kernel(x_ref, gl_ref, yh_ref, ya_ref, yg_ref, wh_ref, wa_ref, wg_ref, wo_ref, o_ref, m_ref):
    ng = MIX // LANE
    ys = [jnp.concatenate([r[g] for g in range(ng)], axis=-1) for r in (yh_ref, ya_ref, yg_ref)]
    ws = (wh_ref, wa_ref, wg_ref)
    cw = 2 * LANE
    gpb = D_MODEL // LANE
    for cc in range(D_MODEL // cw):
        acc = None
        for br in range(N_BRANCH):
            p = _dot(ys[br], ws[br][:, cc * cw:(cc + 1) * cw])
            g0 = br * gpb + 2 * cc
            gl = jnp.concatenate([gl_ref[g0], gl_ref[g0 + 1]], axis=-1).astype(F32)
            t = _sigmoid(gl) * p
            acc = t if acc is None else acc + t
        m_ref[:, cc * cw:(cc + 1) * cw] = acc.astype(BF16)
    o_ref[...] = x_ref[...] + _dot(m_ref[...], wo_ref[...])


def _merge(x2, z, yh, ya, yg, wh, wa, wg, wo):
    n = x2.shape[0]
    tm = TM_PROJ
    ng = MIX // LANE
    ngate = N_BRANCH * D_MODEL // LANE

    def yspec():
        return pl.BlockSpec((ng, tm, LANE), lambda i: (0, i, 0))

    def wspec():
        return _resident((MIX, D_MODEL), lambda i: (0, 0))

    return pl.pallas_call(
        _merge_kernel,
        grid=(n // tm,),
        in_specs=[
            pl.BlockSpec((tm, D_MODEL), lambda i: (i, 0)),
            pl.BlockSpec((ngate, tm, LANE), lambda i: (G_GATE // ngate, i, 0)),
            yspec(), yspec(), yspec(),
            wspec(), wspec(), wspec(),
            _resident((D_MODEL, D_MODEL), lambda i: (0, 0)),
        ],
        out_specs=pl.BlockSpec((tm, D_MODEL), lambda i: (i, 0)),
        out_shape=jax.ShapeDtypeStruct((n, D_MODEL), F32),
        scratch_shapes=[pltpu.VMEM((tm, D_MODEL), BF16)],
        compiler_params=_cparams(("parallel",)),
        name="merge_out",
    )(x2, z, yh, ya, yg, wh, wa, wg, wo)


def _ffn_kernel(x_ref, nw_ref, w1_ref, w2_ref, fw_ref, o_ref, h_ref, ff_ref, *, final):
    x = x_ref[...]
    h_ref[...] = _rms(x, nw_ref[...]).astype(BF16)
    fc = D_MODEL
    for j in range(D_FF // fc):
        a = jnp.maximum(_dot(h_ref[...], w1_ref[:, j * fc:(j + 1) * fc]), 0.0)
        ff_ref[:, j * fc:(j + 1) * fc] = (a * a).astype(BF16)
    y = x + _dot(ff_ref[...], w2_ref[...])
    if final:
        y = _rms(y, fw_ref[...])
    o_ref[...] = y


def _ffn(x2, nw, w1, w2, fw, final):
    n = x2.shape[0]
    tm = TM_PROJ
    return pl.pallas_call(
        functools.partial(_ffn_kernel, final=final),
        grid=(n // tm,),
        in_specs=[
            pl.BlockSpec((tm, D_MODEL), lambda i: (i, 0)),
            _resident((1, D_MODEL), lambda i: (0, 0)),
            _resident((D_MODEL, D_FF), lambda i: (0, 0)),
            _resident((D_FF, D_MODEL), lambda i: (0, 0)),
            _resident((1, D_MODEL), lambda i: (0, 0)),
        ],
        out_specs=pl.BlockSpec((tm, D_MODEL), lambda i: (i, 0)),
        out_shape=jax.ShapeDtypeStruct((n, D_MODEL), F32),
        scratch_shapes=[pltpu.VMEM((tm, D_MODEL), BF16), pltpu.VMEM((tm, D_FF), BF16)],
        compiler_params=_cparams(("parallel",)),
        name="ffn",
    )(x2, nw, w1, w2, fw)


def kernel(x, norm_mix_w, w_in, hgrn_lb_logits, hgrn_norm_w, diff_lam_q1, diff_lam_k1, diff_lam_q2,
           diff_lam_k2, diff_norm_w, gmlp_ln_w, gmlp_ln_b, gmlp_w_s, gmlp_b_s, w_br_hgrn, w_br_attn,
           w_br_gmlp, w_out, norm_ff_w, w_ff1, w_ff2, final_norm_w):
    batch, seq, _ = x.shape
    depth = w_in.shape[0]
    n = batch * seq
    assert n % TM_PROJ == 0 and seq % T_HGRN == 0 and seq % TQ == 0 and n % T_GMLP == 0

    cum = jnp.cumsum(jax.nn.softmax(hgrn_lb_logits.astype(F32), axis=0), axis=0)
    lbs = cum - cum[0:1]
    log_lb = jnp.maximum(jnp.log(lbs), NEG)
    log_1m_lb = jnp.log1p(-lbs)
    one_m_lb = 1.0 - lbs

    split = 9 * MIX
    x2 = x.reshape(n, D_MODEL)
    for l in range(depth):
        w_l = jnp.concatenate([w_in[l][:, split:], w_in[l][:, :split]], axis=1).astype(BF16)
        z = _inproj(x2, norm_mix_w[l].reshape(1, D_MODEL), w_l)

        yh = _hgrn(z, log_lb[l].reshape(1, MIX), log_1m_lb[l].reshape(1, MIX),
                   one_m_lb[l].reshape(1, MIX), hgrn_norm_w[l].reshape(1, DH), batch, seq)
        lam_init = 0.8 - 0.6 * math.exp(-0.3 * l)
        lam_params = jnp.stack([diff_lam_q1[l], diff_lam_k1[l], diff_lam_q2[l], diff_lam_k2[l]]).astype(F32)
        ya = _attn(z, lam_params, diff_norm_w[l].reshape(DH, 1), lam_init, batch, seq)
        yg = _gmlp(z, gmlp_ln_w[l].reshape(MIX // LANE, 1, LANE), gmlp_ln_b[l].reshape(MIX // LANE, 1, LANE),
                   gmlp_w_s[l], gmlp_b_s[l][..., None])

        x2 = _merge(x2, z, yh, ya, yg, w_br_hgrn[l].astype(BF16), w_br_attn[l].astype(BF16),
                    w_br_gmlp[l].astype(BF16), w_out[l].astype(BF16))
        x2 = _ffn(x2, norm_ff_w[l].reshape(1, D_MODEL), w_ff1[l].astype(BF16), w_ff2[l].astype(BF16),
                  final_norm_w.reshape(1, D_MODEL), final=(l == depth - 1))
    return x2.reshape(batch, seq, D_MODEL)
```

```python
import functools
import math

import jax
import jax.numpy as jnp
from jax import lax
from jax.experimental import pallas as pl
from jax.experimental.pallas import tpu as pltpu

F32 = jnp.float32
BF16 = jnp.bfloat16

D_MODEL = 1024
MIX = 512
HEADS = 4
DH = 128
DQK = 64
D_FF = 4 * D_MODEL
N_BRANCH = 3
EPS = 1e-6
GMLP_CHUNK = 128
LANE = 128
IN_COLS = 9 * MIX + N_BRANCH * D_MODEL
NG = IN_COLS // LANE

G_GATE = 0
G_HQ, G_HF, G_HI, G_HG = 24, 28, 32, 36
G_AQ, G_AK, G_AV = 40, 44, 48
G_GU, G_GV = 52, 56

LOG2E = 1.4426950408889634
NEG = -1e30
VMEM_LIMIT = 56 * 1024 * 1024

TM_PROJ = 512
T_HGRN = 512
HGRN_HEADS_PER_STEP = 2
SUB = 16
TQ = 512
TK = 512
CH = 256
T_GMLP = 1024
CW = 512


def _cparams(sem):
    return pltpu.CompilerParams(dimension_semantics=sem, vmem_limit_bytes=VMEM_LIMIT)


def _resident(shape, index_map):
    return pl.BlockSpec(shape, index_map, pipeline_mode=pl.Buffered(1))


def _rms(x, w):
    return x * lax.rsqrt(jnp.mean(x * x, axis=-1, keepdims=True) + EPS) * w


def _gelu(x):
    return 0.5 * x * (1.0 + lax.erf(x * (1.0 / math.sqrt(2.0))))


def _dot(a, b):
    return jnp.dot(a, b, preferred_element_type=F32)


def _dot_nt(a, b):
    return lax.dot_general(a, b, (((1,), (1,)), ((), ())), preferred_element_type=F32)


def _dot_tn(a, b):
    return lax.dot_general(a, b, (((0,), (0,)), ((), ())), preferred_element_type=F32)


def _inproj_kernel(x_ref, nw_ref, w_ref, z_ref, h_ref):
    h_ref[...] = _rms(x_ref[...], nw_ref[...]).astype(BF16)
    gpc = CW // LANE
    for jc in range(IN_COLS // CW):
        r = _dot(h_ref[...], w_ref[:, jc * CW:(jc + 1) * CW])
        for g in range(gpc):
            z_ref[jc * gpc + g] = r[:, g * LANE:(g + 1) * LANE].astype(BF16)


def _inproj(x2, nw, w):
    n = x2.shape[0]
    tm = TM_PROJ
    return pl.pallas_call(
        _inproj_kernel,
        grid=(n // tm,),
        in_specs=[
            pl.BlockSpec((tm, D_MODEL), lambda i: (i, 0)),
            _resident((1, D_MODEL), lambda i: (0, 0)),
            _resident((D_MODEL, IN_COLS), lambda i: (0, 0)),
        ],
        out_specs=pl.BlockSpec((NG, tm, LANE), lambda i: (0, i, 0)),
        out_shape=jax.ShapeDtypeStruct((NG, n, LANE), BF16),
        scratch_shapes=[pltpu.VMEM((tm, D_MODEL), BF16)],
        compiler_params=_cparams(("parallel",)),
        name="inproj",
    )(x2, nw, w)


def _sigmoid(z):
    return 0.5 * jnp.tanh(0.5 * z) + 0.5


def _hgrn_kernel(q_ref, f_ref, i_ref, g_ref, llb_ref, l1m_ref, omlb_ref, nw_ref, o_ref, st_ref, tri_ref):
    t_rows = q_ref.shape[1]

    @pl.when(pl.program_id(2) == 0)
    def _():
        st_ref[...] = jnp.zeros_like(st_ref)
        row = lax.broadcasted_iota(jnp.int32, (t_rows, t_rows), 0)
        col = lax.broadcasted_iota(jnp.int32, (t_rows, t_rows), 1)
        tri_ref[...] = jnp.where(row >= col, 1.0, 0.0).astype(BF16)

    for hd in range(q_ref.shape[0]):
        _hgrn_head(hd, q_ref, f_ref, i_ref, g_ref, llb_ref, l1m_ref, omlb_ref, nw_ref, o_ref, st_ref, tri_ref)


def _hgrn_head(hd, q_ref, f_ref, i_ref, g_ref, llb_ref, l1m_ref, omlb_ref, nw_ref, o_ref, st_ref, tri_ref):
    t_rows = q_ref.shape[1]
    chan = slice(hd * DH, (hd + 1) * DH)

    zq = q_ref[hd].astype(F32)
    zf = f_ref[hd].astype(F32)
    vi = i_ref[hd].astype(F32)
    zg = g_ref[hd].astype(F32)

    qf = zq * _sigmoid(zq)
    log_sig = jnp.minimum(zf, 0.0) - jnp.log(1.0 + jnp.exp2(jnp.abs(zf) * (-LOG2E)))
    c = l1m_ref[:, chan] + log_sig
    a = llb_ref[:, chan]
    logf = jnp.maximum(a, c) + jnp.log(1.0 + jnp.exp2(jnp.abs(a - c) * (-LOG2E)))
    kf = omlb_ref[:, chan] * _sigmoid(-zf)
    lf2 = logf * LOG2E

    hi = lf2.astype(BF16)
    r1 = lf2 - hi.astype(F32)
    mid = r1.astype(BF16)
    lo = (r1 - mid.astype(F32)).astype(BF16)
    c3 = _dot(tri_ref[...], jnp.concatenate([hi, mid, lo], axis=1))
    b = c3[:, :DH] + c3[:, DH:2 * DH] + c3[:, 2 * DH:]

    vb = vi.astype(BF16)
    o = jnp.zeros((t_rows, DH), F32)

    h = SUB
    while h < t_rows:
        nb = t_rows // (2 * h)
        b3 = b.reshape(nb, 2 * h, DH)
        ref = b3[:, h:h + 1, :]
        qp = (qf.reshape(nb, 2 * h, DH)[:, h:, :] * jnp.exp2(b3[:, h:, :] - ref)).astype(BF16)
        kp = (kf.reshape(nb, 2 * h, DH)[:, :h, :] * jnp.exp2(ref - b3[:, :h, :])).astype(BF16)
        a_l = jnp.einsum('nqk,nsk->nqs', qp, kp, preferred_element_type=F32)
        o_l = jnp.einsum('nqs,nsv->nqv', a_l.astype(BF16), vb.reshape(nb, 2 * h, DH)[:, :h, :],
                         preferred_element_type=F32)
        o = o + jnp.concatenate([jnp.zeros_like(o_l), o_l], axis=1).reshape(t_rows, DH)
        h *= 2

    nb = t_rows // SUB
    hs = SUB // 2
    q3 = qf.reshape(nb, SUB, DH)
    b3 = b.reshape(nb, SUB, DH)
    c3s = (b - jnp.log2(kf)).reshape(nb, SUB, DH)
    t8 = lax.broadcasted_iota(jnp.int32, (nb, hs, DH), 1)
    lane = lax.broadcasted_iota(jnp.int32, (nb, hs, DH), 2)
    halves = []
    for half in range(2):
        rows = slice(half * hs, (half + 1) * hs)
        qh, bh = q3[:, rows, :], b3[:, rows, :]
        ah = jnp.zeros((nb, hs, DH), F32)
        for s in range((half + 1) * hs):
            d = bh - c3s[:, s:s + 1, :]
            if s >= half * hs:
                d = jnp.where(t8 >= s - half * hs, d, NEG)
            a_s = jnp.sum(qh * jnp.exp2(d), axis=-1, keepdims=True)
            ah = jnp.where(lane == s, a_s, ah)
        halves.append(ah)
    a_blk = jnp.concatenate(halves, axis=1)[:, :, :SUB].astype(BF16)
    o_blk = jnp.einsum('nts,nsv->ntv', a_blk, vb.reshape(nb, SUB, DH), preferred_element_type=F32)
    o = o + o_blk.reshape(t_rows, DH)

    st = st_ref[hd]
    o = o + _dot_nt((qf * jnp.exp2(b)).astype(BF16), st.astype(BF16))
    b_last = b[t_rows - 1:t_rows, :]
    kd = (kf * jnp.exp2(b_last - b)).astype(BF16)
    st_ref[hd] = jnp.exp2(b_last) * st + _dot_tn(vb, kd)

    y = _rms(o, nw_ref[...]) * (zg * _sigmoid(zg))
    o_ref[hd] = y.astype(BF16)


def _hgrn(z, llb, l1m, omlb, nw, batch, seq):
    t = T_HGRN
    nc = seq // t
    n = batch * seq

    hps = HGRN_HEADS_PER_STEP

    def zspec(g0):
        return pl.BlockSpec((hps, t, LANE), lambda b, h, c: (g0 // hps + h, b * nc + c, 0))

    def pspec():
        return pl.BlockSpec((1, hps * LANE), lambda b, h, c: (0, h))

    return pl.pallas_call(
        _hgrn_kernel,
        grid=(batch, HEADS // hps, nc),
        in_specs=[zspec(G_HQ), zspec(G_HF), zspec(G_HI), zspec(G_HG),
                  pspec(), pspec(), pspec(),
                  pl.BlockSpec((1, LANE), lambda b, h, c: (0, 0))],
        out_specs=pl.BlockSpec((hps, t, LANE), lambda b, h, c: (h, b * nc + c, 0)),
        out_shape=jax.ShapeDtypeStruct((HEADS, n, LANE), BF16),
        scratch_shapes=[pltpu.VMEM((hps, DH, DH), F32), pltpu.VMEM((t, t), BF16)],
        compiler_params=_cparams(("parallel", "parallel", "arbitrary")),
        name="hgrn2",
    )(z, z, z, z, llb, l1m, omlb, nw)


def _attn_kernel(lam_ref, q_ref, k_ref, v_ref, nw_ref, o_ref, sa_ref, sb_ref, m_ref, l_ref, acc_ref, *,
                 lam_init):
    qi = pl.program_id(2)
    tq = q_ref.shape[1]

    lp = lam_ref[...]
    lam = (jnp.exp(jnp.sum(lp[0:1] * lp[1:2], axis=-1, keepdims=True))
           - jnp.exp(jnp.sum(lp[2:3] * lp[3:4], axis=-1, keepdims=True)) + lam_init)

    qt = q_ref[0].astype(F32).T * (DQK ** -0.5 * LOG2E)
    sub = lax.broadcasted_iota(jnp.int32, (LANE, tq), 0)
    qts = jnp.concatenate([jnp.where(sub < DQK, qt, 0.0), jnp.where(sub >= DQK, qt, 0.0)],
                          axis=1).astype(BF16)

    m_ref[...] = jnp.full(m_ref.shape, NEG, F32)
    l_ref[...] = jnp.zeros(l_ref.shape, F32)
    acc_ref[...] = jnp.zeros(acc_ref.shape, F32)

    def stage(jn, sn_ref, jc, sc_ref, masked):
        if jn is not None:
            kt = k_ref[0, pl.ds(pl.multiple_of(jn * TK, TK), TK), :]
        if jc is not None:
            vt = v_ref[0, pl.ds(pl.multiple_of(jc * TK, TK), TK), :]
        for c in range(2 * tq // CH):
            ls = slice(c * CH, (c + 1) * CH)
            if jn is not None:
                s = _dot(kt, qts[:, ls])
                sn_ref[0:TK, ls] = s
                sn_ref[TK:TK + 1, ls] = jnp.max(s, axis=0, keepdims=True)
            if jc is not None:
                s = sc_ref[0:TK, ls]
                if masked:
                    krow = lax.broadcasted_iota(jnp.int32, (TK, CH), 0)
                    qcol = (lax.broadcasted_iota(jnp.int32, (TK, CH), 1) + c * CH) & (tq - 1)
                    s = jnp.where(krow <= qcol, s, NEG)
                    tile_max = jnp.max(s, axis=0, keepdims=True)
                else:
                    tile_max = sc_ref[TK:TK + 1, ls]
                m_prev = m_ref[:, ls]
                m_cur = jnp.maximum(m_prev, tile_max)
                alpha = jnp.exp2(m_prev - m_cur)
                p = jnp.exp2(s - m_cur)
                l_ref[:, ls] = alpha * l_ref[:, ls] + jnp.sum(p, axis=0, keepdims=True)
                acc_ref[:, ls] = alpha * acc_ref[:, ls] + _dot_tn(vt, p.astype(BF16))
                m_ref[:, ls] = m_cur

    stage(0, sa_ref, None, None, False)

    def pair_body(t, carry):
        stage(2 * t + 1, sb_ref, 2 * t, sa_ref, False)
        stage(2 * t + 2, sa_ref, 2 * t + 1, sb_ref, False)
        return carry

    lax.fori_loop(0, qi // 2, pair_body, 0)

    @pl.when(qi % 2 == 0)
    def _():
        stage(None, None, qi, sa_ref, True)

    @pl.when(qi % 2 == 1)
    def _():
        stage(qi, sb_ref, qi - 1, sa_ref, False)
        stage(None, None, qi, sb_ref, True)

    on = acc_ref[...] / l_ref[...]
    ot = on[:, :tq] - lam * on[:, tq:]
    yt = ot * lax.rsqrt(jnp.mean(ot * ot, axis=0, keepdims=True) + EPS) * nw_ref[...]
    o_ref[0] = (yt.T * (1.0 - lam_init)).astype(BF16)


def _attn(z, lam_params, nw, lam_init, batch, seq):
    assert TQ == TK
    nq = seq // TQ
    n = batch * seq
    return pl.pallas_call(
        functools.partial(_attn_kernel, lam_init=lam_init),
        grid=(batch, HEADS, nq),
        in_specs=[
            pl.BlockSpec((4, DQK), lambda b, h, i: (0, 0)),
            pl.BlockSpec((1, TQ, LANE), lambda b, h, i: (G_AQ + h, b * nq + i, 0)),
            pl.BlockSpec((1, seq, LANE), lambda b, h, i: (G_AK + h, b, 0)),
            pl.BlockSpec((1, seq, LANE), lambda b, h, i: (G_AV + h, b, 0)),
            pl.BlockSpec((DH, 1), lambda b, h, i: (0, 0)),
        ],
        out_specs=pl.BlockSpec((1, TQ, LANE), lambda b, h, i: (h, b * nq + i, 0)),
        out_shape=jax.ShapeDtypeStruct((HEADS, n, LANE), BF16),
        scratch_shapes=[pltpu.VMEM((TK + 8, 2 * TQ), F32), pltpu.VMEM((TK + 8, 2 * TQ), F32),
                        pltpu.VMEM((1, 2 * TQ), F32), pltpu.VMEM((1, 2 * TQ), F32),
                        pltpu.VMEM((DH, 2 * TQ), F32)],
        compiler_params=_cparams(("parallel", "parallel", "arbitrary")),
        name="diff_attn",
    )(lam_params, z, z, z, nw)


def _gmlp_kernel(u_ref, v_ref, lnw_ref, lnb_ref, ws_ref, bs_ref, o_ref):
    t_rows = u_ref.shape[1]
    ng = MIX // LANE
    vs = [_gelu(v_ref[g].astype(F32)) for g in range(ng)]
    mu = sum(jnp.sum(v, axis=-1, keepdims=True) for v in vs) * (1.0 / MIX)
    var = sum(jnp.sum(jnp.square(v - mu), axis=-1, keepdims=True) for v in vs) * (1.0 / MIX)
    rs = lax.rsqrt(var + EPS)
    row = lax.broadcasted_iota(jnp.int32, (GMLP_CHUNK, GMLP_CHUNK), 0)
    col = lax.broadcasted_iota(jnp.int32, (GMLP_CHUNK, GMLP_CHUNK), 1)
    for g in range(ng):
        vn = ((vs[g] - mu) * rs * lnw_ref[g] + lnb_ref[g]).astype(BF16)
        w = jnp.where(row >= col, ws_ref[g], 0.0).astype(BF16)
        for n in range(t_rows // GMLP_CHUNK):
            sl = slice(n * GMLP_CHUNK, (n + 1) * GMLP_CHUNK)
            mixed = _dot(w, vn[sl]) + bs_ref[g]
            o_ref[g, sl, :] = (_gelu(u_ref[g, sl, :].astype(F32)) * mixed).astype(BF16)


def _gmlp(z, lnw, lnb, ws, bs):
    n = z.shape[1]
    t = T_GMLP
    ng = MIX // LANE
    return pl.pallas_call(
        _gmlp_kernel,
        grid=(n // t,),
        in_specs=[
            pl.BlockSpec((ng, t, LANE), lambda i: (G_GU // ng, i, 0)),
            pl.BlockSpec((ng, t, LANE), lambda i: (G_GV // ng, i, 0)),
            pl.BlockSpec((ng, 1, LANE), lambda i: (0, 0, 0)),
            pl.BlockSpec((ng, 1, LANE), lambda i: (0, 0, 0)),
            pl.BlockSpec((ng, GMLP_CHUNK, GMLP_CHUNK), lambda i: (0, 0, 0)),
            pl.BlockSpec((ng, GMLP_CHUNK, 1), lambda i: (0, 0, 0)),
        ],
        out_specs=pl.BlockSpec((ng, t, LANE), lambda i: (0, i, 0)),
        out_shape=jax.ShapeDtypeStruct((ng, n, LANE), BF16),
        compiler_params=_cparams(("parallel",)),
        name="gmlp",
    )(z, z, lnw, lnb, ws, bs)


def _merge_kernel(x_ref, gl_ref, yh_ref, ya_ref, yg_ref, wh_ref, wa_ref, wg_ref, wo_ref, o_ref, m_ref):
    ng = MIX // LANE
    ys = [jnp.concatenate([r[g] for g in range(ng)], axis=-1) for r in (yh_ref, ya_ref, yg_ref)]
    ws = (wh_ref, wa_ref, wg_ref)
    cw = 2 * LANE
    gpb = D_MODEL // LANE
    for cc in range(D_MODEL // cw):
        acc = None
        for br in range(N_BRANCH):
            p = _dot(ys[br], ws[br][:, cc * cw:(cc + 1) * cw])
            g0 = br * gpb + 2 * cc
            gl = jnp.concatenate([gl_ref[g0], gl_ref[g0 + 1]], axis=-1).astype(F32)
            t = _sigmoid(gl) * p
            acc = t if acc is None else acc + t
        m_ref[:, cc * cw:(cc + 1) * cw] = acc.astype(BF16)
    o_ref[...] = x_ref[...] + _dot(m_ref[...], wo_ref[...])


def _merge(x2, z, yh, ya, yg, wh, wa, wg, wo):
    n = x2.shape[0]
    tm = TM_PROJ
    ng = MIX // LANE
    ngate = N_BRANCH * D_MODEL // LANE

    def yspec():
        return pl.BlockSpec((ng, tm, LANE), lambda i: (0, i, 0))

    def wspec():
        return _resident((MIX, D_MODEL), lambda i: (0, 0))

    return pl.pallas_call(
        _merge_kernel,
        grid=(n // tm,),
        in_specs=[
            pl.BlockSpec((tm, D_MODEL), lambda i: (i, 0)),
            pl.BlockSpec((ngate, tm, LANE), lambda i: (G_GATE // ngate, i, 0)),
            yspec(), yspec(), yspec(),
            wspec(), wspec(), wspec(),
            _resident((D_MODEL, D_MODEL), lambda i: (0, 0)),
        ],
        out_specs=pl.BlockSpec((tm, D_MODEL), lambda i: (i, 0)),
        out_shape=jax.ShapeDtypeStruct((n, D_MODEL), F32),
        scratch_shapes=[pltpu.VMEM((tm, D_MODEL), BF16)],
        compiler_params=_cparams(("parallel",)),
        name="merge_out",
    )(x2, z, yh, ya, yg, wh, wa, wg, wo)


def _ffn_kernel(x_ref, nw_ref, w1_ref, w2_ref, fw_ref, o_ref, h_ref, ff_ref, *, final):
    x = x_ref[...]
    h_ref[...] = _rms(x, nw_ref[...]).astype(BF16)
    fc = D_MODEL
    for j in range(D_FF // fc):
        a = jnp.maximum(_dot(h_ref[...], w1_ref[:, j * fc:(j + 1) * fc]), 0.0)
        ff_ref[:, j * fc:(j + 1) * fc] = (a * a).astype(BF16)
    y = x + _dot(ff_ref[...], w2_ref[...])
    if final:
        y = _rms(y, fw_ref[...])
    o_ref[...] = y


def _ffn(x2, nw, w1, w2, fw, final):
    n = x2.shape[0]
    tm = TM_PROJ
    return pl.pallas_call(
        functools.partial(_ffn_kernel, final=final),
        grid=(n // tm,),
        in_specs=[
            pl.BlockSpec((tm, D_MODEL), lambda i: (i, 0)),
            _resident((1, D_MODEL), lambda i: (0, 0)),
            _resident((D_MODEL, D_FF), lambda i: (0, 0)),
            _resident((D_FF, D_MODEL), lambda i: (0, 0)),
            _resident((1, D_MODEL), lambda i: (0, 0)),
        ],
        out_specs=pl.BlockSpec((tm, D_MODEL), lambda i: (i, 0)),
        out_shape=jax.ShapeDtypeStruct((n, D_MODEL), F32),
        scratch_shapes=[pltpu.VMEM((tm, D_MODEL), BF16), pltpu.VMEM((tm, D_FF), BF16)],
        compiler_params=_cparams(("parallel",)),
        name="ffn",
    )(x2, nw, w1, w2, fw)


def kernel(x, norm_mix_w, w_in, hgrn_lb_logits, hgrn_norm_w, diff_lam_q1, diff_lam_k1, diff_lam_q2,
           diff_lam_k2, diff_norm_w, gmlp_ln_w, gmlp_ln_b, gmlp_w_s, gmlp_b_s, w_br_hgrn, w_br_attn,
           w_br_gmlp, w_out, norm_ff_w, w_ff1, w_ff2, final_norm_w):
    batch, seq, _ = x.shape
    depth = w_in.shape[0]
    n = batch * seq
    assert n % TM_PROJ == 0 and seq % T_HGRN == 0 and seq % TQ == 0 and n % T_GMLP == 0

    cum = jnp.cumsum(jax.nn.softmax(hgrn_lb_logits.astype(F32), axis=0), axis=0)
    lbs = cum - cum[0:1]
    log_lb = jnp.maximum(jnp.log(lbs), NEG)
    log_1m_lb = jnp.log1p(-lbs)
    one_m_lb = 1.0 - lbs

    split = 9 * MIX
    x2 = x.reshape(n, D_MODEL)
    for l in range(depth):
        w_l = jnp.concatenate([w_in[l][:, split:], w_in[l][:, :split]], axis=1).astype(BF16)
        z = _inproj(x2, norm_mix_w[l].reshape(1, D_MODEL), w_l)

        yh = _hgrn(z, log_lb[l].reshape(1, MIX), log_1m_lb[l].reshape(1, MIX),
                   one_m_lb[l].reshape(1, MIX), hgrn_norm_w[l].reshape(1, DH), batch, seq)
        lam_init = 0.8 - 0.6 * math.exp(-0.3 * l)
        lam_params = jnp.stack([diff_lam_q1[l], diff_lam_k1[l], diff_lam_q2[l], diff_lam_k2[l]]).astype(F32)
        ya = _attn(z, lam_params, diff_norm_w[l].reshape(DH, 1), lam_init, batch, seq)
        yg = _gmlp(z, gmlp_ln_w[l].reshape(MIX // LANE, 1, LANE), gmlp_ln_b[l].reshape(MIX // LANE, 1, LANE),
                   gmlp_w_s[l], gmlp_b_s[l][..., None])

        x2 = _merge(x2, z, yh, ya, yg, w_br_hgrn[l].astype(BF16), w_br_attn[l].astype(BF16),
                    w_br_gmlp[l].astype(BF16), w_out[l].astype(BF16))
        x2 = _ffn(x2, norm_ff_w[l].reshape(1, D_MODEL), w_ff1[l].astype(BF16), w_ff2[l].astype(BF16),
                  final_norm_w.reshape(1, D_MODEL), final=(l == depth - 1))
    return x2.reshape(batch, seq, D_MODEL)
```

```python
import functools
import math

import jax
import jax.numpy as jnp
from jax import lax
from jax.experimental import pallas as pl
from jax.experimental.pallas import tpu as pltpu

F32 = jnp.float32
BF16 = jnp.bfloat16

D_MODEL = 1024
MIX = 512
HEADS = 4
DH = 128
DQK = 64
D_FF = 4 * D_MODEL
N_BRANCH = 3
EPS = 1e-6
GMLP_CHUNK = 128
LANE = 128
IN_COLS = 9 * MIX + N_BRANCH * D_MODEL
NG = IN_COLS // LANE

G_GATE = 0
G_HQ, G_HF, G_HI, G_HG = 24, 28, 32, 36
G_AQ, G_AK, G_AV = 40, 44, 48
G_GU, G_GV = 52, 56

LOG2E = 1.4426950408889634
NEG = -1e30
VMEM_LIMIT = 56 * 1024 * 1024

TM_PROJ = 512
T_HGRN = 512
HGRN_HEADS_PER_STEP = 2
SUB = 16
TQ = 512
TK = 512
CH = 256
T_GMLP = 1024
CW = 512


def _cparams(sem):
    return pltpu.CompilerParams(dimension_semantics=sem, vmem_limit_bytes=VMEM_LIMIT)


def _resident(shape, index_map):
    return pl.BlockSpec(shape, index_map, pipeline_mode=pl.Buffered(1))


def _rms(x, w):
    return x * lax.rsqrt(jnp.mean(x * x, axis=-1, keepdims=True) + EPS) * w


def _gelu(x):
    return 0.5 * x * (1.0 + lax.erf(x * (1.0 / math.sqrt(2.0))))


def _dot(a, b):
    return jnp.dot(a, b, preferred_element_type=F32)


def _dot_nt(a, b):
    return lax.dot_general(a, b, (((1,), (1,)), ((), ())), preferred_element_type=F32)


def _dot_tn(a, b):
    return lax.dot_general(a, b, (((0,), (0,)), ((), ())), preferred_element_type=F32)


def _inproj_kernel(x_ref, nw_ref, w_ref, z_ref, h_ref):
    h_ref[...] = _rms(x_ref[...], nw_ref[...]).astype(BF16)
    gpc = CW // LANE
    for jc in range(IN_COLS // CW):
        r = _dot(h_ref[...], w_ref[:, jc * CW:(jc + 1) * CW])
        for g in range(gpc):
            z_ref[jc * gpc + g] = r[:, g * LANE:(g + 1) * LANE].astype(BF16)


def _inproj(x2, nw, w):
    n = x2.shape[0]
    tm = TM_PROJ
    return pl.pallas_call(
        _inproj_kernel,
        grid=(n // tm,),
        in_specs=[
            pl.BlockSpec((tm, D_MODEL), lambda i: (i, 0)),
            _resident((1, D_MODEL), lambda i: (0, 0)),
            _resident((D_MODEL, IN_COLS), lambda i: (0, 0)),
        ],
        out_specs=pl.BlockSpec((NG, tm, LANE), lambda i: (0, i, 0)),
        out_shape=jax.ShapeDtypeStruct((NG, n, LANE), BF16),
        scratch_shapes=[pltpu.VMEM((tm, D_MODEL), BF16)],
        compiler_params=_cparams(("parallel",)),
        name="inproj",
    )(x2, nw, w)


def _sigmoid(z):
    return 0.5 * jnp.tanh(0.5 * z) + 0.5


def _hgrn_kernel(q_ref, f_ref, i_ref, g_ref, llb_ref, l1m_ref, omlb_ref, nw_ref, o_ref, st_ref, tri_ref):
    t_rows = q_ref.shape[1]

    @pl.when(pl.program_id(2) == 0)
    def _():
        st_ref[...] = jnp.zeros_like(st_ref)
        row = lax.broadcasted_iota(jnp.int32, (t_rows, t_rows), 0)
        col = lax.broadcasted_iota(jnp.int32, (t_rows, t_rows), 1)
        tri_ref[...] = jnp.where(row >= col, 1.0, 0.0).astype(BF16)

    for hd in range(q_ref.shape[0]):
        _hgrn_head(hd, q_ref, f_ref, i_ref, g_ref, llb_ref, l1m_ref, omlb_ref, nw_ref, o_ref, st_ref, tri_ref)


def _hgrn_head(hd, q_ref, f_ref, i_ref, g_ref, llb_ref, l1m_ref, omlb_ref, nw_ref, o_ref, st_ref, tri_ref):
    t_rows = q_ref.shape[1]
    chan = slice(hd * DH, (hd + 1) * DH)

    zq = q_ref[hd].astype(F32)
    zf = f_ref[hd].astype(F32)
    vi = i_ref[hd].astype(F32)
    zg = g_ref[hd].astype(F32)

    qf = zq * _sigmoid(zq)
    log_sig = jnp.minimum(zf, 0.0) - jnp.log(1.0 + jnp.exp2(jnp.abs(zf) * (-LOG2E)))
    c = l1m_ref[:, chan] + log_sig
    a = llb_ref[:, chan]
    logf = jnp.maximum(a, c) + jnp.log(1.0 + jnp.exp2(jnp.abs(a - c) * (-LOG2E)))
    kf = omlb_ref[:, chan] * _sigmoid(-zf)
    lf2 = logf * LOG2E

    hi = lf2.astype(BF16)
    r1 = lf2 - hi.astype(F32)
    mid = r1.astype(BF16)
    lo = (r1 - mid.astype(F32)).astype(BF16)
    c3 = _dot(tri_ref[...], jnp.concatenate([hi, mid, lo], axis=1))
    b = c3[:, :DH] + c3[:, DH:2 * DH] + c3[:, 2 * DH:]

    vb = vi.astype(BF16)
    o = jnp.zeros((t_rows, DH), F32)

    h = SUB
    while h < t_rows:
        nb = t_rows // (2 * h)
        b3 = b.reshape(nb, 2 * h, DH)
        ref = b3[:, h:h + 1, :]
        qp = (qf.reshape(nb, 2 * h, DH)[:, h:, :] * jnp.exp2(b3[:, h:, :] - ref)).astype(BF16)
        kp = (kf.reshape(nb, 2 * h, DH)[:, :h, :] * jnp.exp2(ref - b3[:, :h, :])).astype(BF16)
        a_l = jnp.einsum('nqk,nsk->nqs', qp, kp, preferred_element_type=F32)
        o_l = jnp.einsum('nqs,nsv->nqv', a_l.astype(BF16), vb.reshape(nb, 2 * h, DH)[:, :h, :],
                         preferred_element_type=F32)
        o = o + jnp.concatenate([jnp.zeros_like(o_l), o_l], axis=1).reshape(t_rows, DH)
        h *= 2

    nb = t_rows // SUB
    hs = SUB // 2
    q3 = qf.reshape(nb, SUB, DH)
    b3 = b.reshape(nb, SUB, DH)
    c3s = (b - jnp.log2(kf)).reshape(nb, SUB, DH)
    t8 = lax.broadcasted_iota(jnp.int32, (nb, hs, DH), 1)
    lane = lax.broadcasted_iota(jnp.int32, (nb, hs, DH), 2)
    halves = []
    for half in range(2):
        rows = slice(half * hs, (half + 1) * hs)
        qh, bh = q3[:, rows, :], b3[:, rows, :]
        ah = jnp.zeros((nb, hs, DH), F32)
        for s in range((half + 1) * hs):
            d = bh - c3s[:, s:s + 1, :]
            if s >= half * hs:
                d = jnp.where(t8 >= s - half * hs, d, NEG)
            a_s = jnp.sum(qh * jnp.exp2(d), axis=-1, keepdims=True)
            ah = jnp.where(lane == s, a_s, ah)
        halves.append(ah)
    a_blk = jnp.concatenate(halves, axis=1)[:, :, :SUB].astype(BF16)
    o_blk = jnp.einsum('nts,nsv->ntv', a_blk, vb.reshape(nb, SUB, DH), preferred_element_type=F32)
    o = o + o_blk.reshape(t_rows, DH)

    st = st_ref[hd]
    o = o + _dot_nt((qf * jnp.exp2(b)).astype(BF16), st.astype(BF16))
    b_last = b[t_rows - 1:t_rows, :]
    kd = (kf * jnp.exp2(b_last - b)).astype(BF16)
    st_ref[hd] = jnp.exp2(b_last) * st + _dot_tn(vb, kd)

    y = _rms(o, nw_ref[...]) * (zg * _sigmoid(zg))
    o_ref[hd] = y.astype(BF16)


def _hgrn(z, llb, l1m, omlb, nw, batch, seq):
    t = T_HGRN
    nc = seq // t
    n = batch * seq

    hps = HGRN_HEADS_PER_STEP

    def zspec(g0):
        return pl.BlockSpec((hps, t, LANE), lambda b, h, c: (g0 // hps + h, b * nc + c, 0))

    def pspec():
        return pl.BlockSpec((1, hps * LANE), lambda b, h, c: (0, h))

    return pl.pallas_call(
        _hgrn_kernel,
        grid=(batch, HEADS // hps, nc),
        in_specs=[zspec(G_HQ), zspec(G_HF), zspec(G_HI), zspec(G_HG),
                  pspec(), pspec(), pspec(),
                  pl.BlockSpec((1, LANE), lambda b, h, c: (0, 0))],
        out_specs=pl.BlockSpec((hps, t, LANE), lambda b, h, c: (h, b * nc + c, 0)),
        out_shape=jax.ShapeDtypeStruct((HEADS, n, LANE), BF16),
        scratch_shapes=[pltpu.VMEM((hps, DH, DH), F32), pltpu.VMEM((t, t), BF16)],
        compiler_params=_cparams(("parallel", "parallel", "arbitrary")),
        name="hgrn2",
    )(z, z, z, z, llb, l1m, omlb, nw)


def _attn_kernel(lam_ref, q_ref, k_ref, v_ref, nw_ref, o_ref, sa_ref, sb_ref, m_ref, l_ref, acc_ref, *,
                 lam_init):
    qi = pl.program_id(2)
    tq = q_ref.shape[1]

    lp = lam_ref[...]
    lam = (jnp.exp(jnp.sum(lp[0:1] * lp[1:2], axis=-1, keepdims=True))
           - jnp.exp(jnp.sum(lp[2:3] * lp[3:4], axis=-1, keepdims=True)) + lam_init)

    qt = q_ref[0].astype(F32).T * (DQK ** -0.5 * LOG2E)
    sub = lax.broadcasted_iota(jnp.int32, (LANE, tq), 0)
    qts = jnp.concatenate([jnp.where(sub < DQK, qt, 0.0), jnp.where(sub >= DQK, qt, 0.0)],
                          axis=1).astype(BF16)

    m_ref[...] = jnp.full(m_ref.shape, NEG, F32)
    l_ref[...] = jnp.zeros(l_ref.shape, F32)
    acc_ref[...] = jnp.zeros(acc_ref.shape, F32)

    def stage(jn, sn_ref, jc, sc_ref, masked):
        if jn is not None:
            kt = k_ref[0, pl.ds(pl.multiple_of(jn * TK, TK), TK), :]
        if jc is not None:
            vt = v_ref[0, pl.ds(pl.multiple_of(jc * TK, TK), TK), :]
        for c in range(2 * tq // CH):
            ls = slice(c * CH, (c + 1) * CH)
            if jn is not None:
                s = _dot(kt, qts[:, ls])
                sn_ref[0:TK, ls] = s
                sn_ref[TK:TK + 1, ls] = jnp.max(s, axis=0, keepdims=True)
            if jc is not None:
                s = sc_ref[0:TK, ls]
                if masked:
                    krow = lax.broadcasted_iota(jnp.int32, (TK, CH), 0)
                    qcol = (lax.broadcasted_iota(jnp.int32, (TK, CH), 1) + c * CH) & (tq - 1)
                    s = jnp.where(krow <= qcol, s, NEG)
                    tile_max = jnp.max(s, axis=0, keepdims=True)
                else:
                    tile_max = sc_ref[TK:TK + 1, ls]
                m_prev = m_ref[:, ls]
                m_cur = jnp.maximum(m_prev, tile_max)
                alpha = jnp.exp2(m_prev - m_cur)
                p = jnp.exp2(s - m_cur)
                l_ref[:, ls] = alpha * l_ref[:, ls] + jnp.sum(p, axis=0, keepdims=True)
                acc_ref[:, ls] = alpha * acc_ref[:, ls] + _dot_tn(vt, p.astype(BF16))
                m_ref[:, ls] = m_cur

    stage(0, sa_ref, None, None, False)

    def pair_body(t, carry):
        stage(2 * t + 1, sb_ref, 2 * t, sa_ref, False)
        stage(2 * t + 2, sa_ref, 2 * t + 1, sb_ref, False)
        return carry

    def quad_body(t, carry):
        pair_body(2 * t, carry)
        pair_body(2 * t + 1, carry)
        return carry

    lax.fori_loop(0, qi // 4, quad_body, 0)
    lax.fori_loop(2 * (qi // 4), qi // 2, pair_body, 0)

    @pl.when(qi % 2 == 0)
    def _():
        stage(None, None, qi, sa_ref, True)

    @pl.when(qi % 2 == 1)
    def _():
        stage(qi, sb_ref, qi - 1, sa_ref, False)
        stage(None, None, qi, sb_ref, True)

    on = acc_ref[...] / l_ref[...]
    ot = on[:, :tq] - lam * on[:, tq:]
    yt = ot * lax.rsqrt(jnp.mean(ot * ot, axis=0, keepdims=True) + EPS) * nw_ref[...]
    o_ref[0] = (yt.T * (1.0 - lam_init)).astype(BF16)


def _attn(z, lam_params, nw, lam_init, batch, seq):
    assert TQ == TK
    nq = seq // TQ
    n = batch * seq
    return pl.pallas_call(
        functools.partial(_attn_kernel, lam_init=lam_init),
        grid=(batch, HEADS, nq),
        in_specs=[
            pl.BlockSpec((4, DQK), lambda b, h, i: (0, 0)),
            pl.BlockSpec((1, TQ, LANE), lambda b, h, i: (G_AQ + h, b * nq + i, 0)),
            pl.BlockSpec((1, seq, LANE), lambda b, h, i: (G_AK + h, b, 0)),
            pl.BlockSpec((1, seq, LANE), lambda b, h, i: (G_AV + h, b, 0)),
            pl.BlockSpec((DH, 1), lambda b, h, i: (0, 0)),
        ],
        out_specs=pl.BlockSpec((1, TQ, LANE), lambda b, h, i: (h, b * nq + i, 0)),
        out_shape=jax.ShapeDtypeStruct((HEADS, n, LANE), BF16),
        scratch_shapes=[pltpu.VMEM((TK + 8, 2 * TQ), F32), pltpu.VMEM((TK + 8, 2 * TQ), F32),
                        pltpu.VMEM((1, 2 * TQ), F32), pltpu.VMEM((1, 2 * TQ), F32),
                        pltpu.VMEM((DH, 2 * TQ), F32)],
        compiler_params=_cparams(("parallel", "parallel", "arbitrary")),
        name="diff_attn",
    )(lam_params, z, z, z, nw)


def _gmlp_kernel(u_ref, v_ref, lnw_ref, lnb_ref, ws_ref, bs_ref, o_ref):
    t_rows = u_ref.shape[1]
    ng = MIX // LANE
    vs = [_gelu(v_ref[g].astype(F32)) for g in range(ng)]
    mu = sum(jnp.sum(v, axis=-1, keepdims=True) for v in vs) * (1.0 / MIX)
    var = sum(jnp.sum(jnp.square(v - mu), axis=-1, keepdims=True) for v in vs) * (1.0 / MIX)
    rs = lax.rsqrt(var + EPS)
    row = lax.broadcasted_iota(jnp.int32, (GMLP_CHUNK, GMLP_CHUNK), 0)
    col = lax.broadcasted_iota(jnp.int32, (GMLP_CHUNK, GMLP_CHUNK), 1)
    for g in range(ng):
        vn = ((vs[g] - mu) * rs * lnw_ref[g] + lnb_ref[g]).astype(BF16)
        w = jnp.where(row >= col, ws_ref[g], 0.0).astype(BF16)
        for n in range(t_rows // GMLP_CHUNK):
            sl = slice(n * GMLP_CHUNK, (n + 1) * GMLP_CHUNK)
            mixed = _dot(w, vn[sl]) + bs_ref[g]
            o_ref[g, sl, :] = (_gelu(u_ref[g, sl, :].astype(F32)) * mixed).astype(BF16)


def _gmlp(z, lnw, lnb, ws, bs):
    n = z.shape[1]
    t = T_GMLP
    ng = MIX // LANE
    return pl.pallas_call(
        _gmlp_kernel,
        grid=(n // t,),
        in_specs=[
            pl.BlockSpec((ng, t, LANE), lambda i: (G_GU // ng, i, 0)),
            pl.BlockSpec((ng, t, LANE), lambda i: (G_GV // ng, i, 0)),
            pl.BlockSpec((ng, 1, LANE), lambda i: (0, 0, 0)),
            pl.BlockSpec((ng, 1, LANE), lambda i: (0, 0, 0)),
            pl.BlockSpec((ng, GMLP_CHUNK, GMLP_CHUNK), lambda i: (0, 0, 0)),
            pl.BlockSpec((ng, GMLP_CHUNK, 1), lambda i: (0, 0, 0)),
        ],
        out_specs=pl.BlockSpec((ng, t, LANE), lambda i: (0, i, 0)),
        out_shape=jax.ShapeDtypeStruct((ng, n, LANE), BF16),
        compiler_params=_cparams(("parallel",)),
        name="gmlp",
    )(z, z, lnw, lnb, ws, bs)


def _merge_kernel(x_ref, gl_ref, yh_ref, ya_ref, yg_ref, wh_ref, wa_ref, wg_ref, wo_ref, o_ref, m_ref):
    ng = MIX // LANE
    ys = [jnp.concatenate([r[g] for g in range(ng)], axis=-1) for r in (yh_ref, ya_ref, yg_ref)]
    ws = (wh_ref, wa_ref, wg_ref)
    cw = 2 * LANE
    gpb = D_MODEL // LANE
    for cc in range(D_MODEL // cw):
        acc = None
        for br in range(N_BRANCH):
            p = _dot(ys[br], ws[br][:, cc * cw:(cc + 1) * cw])
            g0 = br * gpb + 2 * cc
            gl = jnp.concatenate([gl_ref[g0], gl_ref[g0 + 1]], axis=-1).astype(F32)
            t = _sigmoid(gl) * p
            acc = t if acc is None else acc + t
        m_ref[:, cc * cw:(cc + 1) * cw] = acc.astype(BF16)
    o_ref[...] = x_ref[...] + _dot(m_ref[...], wo_ref[...])


def _merge(x2, z, yh, ya, yg, wh, wa, wg, wo):
    n = x2.shape[0]
    tm = TM_PROJ
    ng = MIX // LANE
    ngate = N_BRANCH * D_MODEL // LANE

    def yspec():
        return pl.BlockSpec((ng, tm, LANE), lambda i: (0, i, 0))

    def wspec():
        return _resident((MIX, D_MODEL), lambda i: (0, 0))

    return pl.pallas_call(
        _merge_kernel,
        grid=(n // tm,),
        in_specs=[
            pl.BlockSpec((tm, D_MODEL), lambda i: (i, 0)),
            pl.BlockSpec((ngate, tm, LANE), lambda i: (G_GATE // ngate, i, 0)),
            yspec(), yspec(), yspec(),
            wspec(), wspec(), wspec(),
            _resident((D_MODEL, D_MODEL), lambda i: (0, 0)),
        ],
        out_specs=pl.BlockSpec((tm, D_MODEL), lambda i: (i, 0)),
        out_shape=jax.ShapeDtypeStruct((n, D_MODEL), F32),
        scratch_shapes=[pltpu.VMEM((tm, D_MODEL), BF16)],
        compiler_params=_cparams(("parallel",)),
        name="merge_out",
    )(x2, z, yh, ya, yg, wh, wa, wg, wo)


def _ffn_kernel(x_ref, nw_ref, w1_ref, w2_ref, fw_ref, o_ref, h_ref, ff_ref, *, final):
    x = x_ref[...]
    h_ref[...] = _rms(x, nw_ref[...]).astype(BF16)
    fc = D_MODEL
    for j in range(D_FF // fc):
        a = jnp.maximum(_dot(h_ref[...], w1_ref[:, j * fc:(j + 1) * fc]), 0.0)
        ff_ref[:, j * fc:(j + 1) * fc] = (a * a).astype(BF16)
    y = x + _dot(ff_ref[...], w2_ref[...])
    if final:
        y = _rms(y, fw_ref[...])
    o_ref[...] = y


def _ffn(x2, nw, w1, w2, fw, final):
    n = x2.shape[0]
    tm = TM_PROJ
    return pl.pallas_call(
        functools.partial(_ffn_kernel, final=final),
        grid=(n // tm,),
        in_specs=[
            pl.BlockSpec((tm, D_MODEL), lambda i: (i, 0)),
            _resident((1, D_MODEL), lambda i: (0, 0)),
            _resident((D_MODEL, D_FF), lambda i: (0, 0)),
            _resident((D_FF, D_MODEL), lambda i: (0, 0)),
            _resident((1, D_MODEL), lambda i: (0, 0)),
        ],
        out_specs=pl.BlockSpec((tm, D_MODEL), lambda i: (i, 0)),
        out_shape=jax.ShapeDtypeStruct((n, D_MODEL), F32),
        scratch_shapes=[pltpu.VMEM((tm, D_MODEL), BF16), pltpu.VMEM((tm, D_FF), BF16)],
        compiler_params=_cparams(("parallel",)),
        name="ffn",
    )(x2, nw, w1, w2, fw)


def kernel(x, norm_mix_w, w_in, hgrn_lb_logits, hgrn_norm_w, diff_lam_q1, diff_lam_k1, diff_lam_q2,
           diff_lam_k2, diff_norm_w, gmlp_ln_w, gmlp_ln_b, gmlp_w_s, gmlp_b_s, w_br_hgrn, w_br_attn,
           w_br_gmlp, w_out, norm_ff_w, w_ff1, w_ff2, final_norm_w):
    batch, seq, _ = x.shape
    depth = w_in.shape[0]
    n = batch * seq
    assert n % TM_PROJ == 0 and seq % T_HGRN == 0 and seq % TQ == 0 and n % T_GMLP == 0

    cum = jnp.cumsum(jax.nn.softmax(hgrn_lb_logits.astype(F32), axis=0), axis=0)
    lbs = cum - cum[0:1]
    log_lb = jnp.maximum(jnp.log(lbs), NEG)
    log_1m_lb = jnp.log1p(-lbs)
    one_m_lb = 1.0 - lbs

    split = 9 * MIX
    x2 = x.reshape(n, D_MODEL)
    for l in range(depth):
        w_l = jnp.concatenate([w_in[l][:, split:], w_in[l][:, :split]], axis=1).astype(BF16)
        z = _inproj(x2, norm_mix_w[l].reshape(1, D_MODEL), w_l)

        yh = _hgrn(z, log_lb[l].reshape(1, MIX), log_1m_lb[l].reshape(1, MIX),
                   one_m_lb[l].reshape(1, MIX), hgrn_norm_w[l].reshape(1, DH), batch, seq)
        lam_init = 0.8 - 0.6 * math.exp(-0.3 * l)
        lam_params = jnp.stack([diff_lam_q1[l], diff_lam_k1[l], diff_lam_q2[l], diff_lam_k2[l]]).astype(F32)
        ya = _attn(z, lam_params, diff_norm_w[l].reshape(DH, 1), lam_init, batch, seq)
        yg = _gmlp(z, gmlp_ln_w[l].reshape(MIX // LANE, 1, LANE), gmlp_ln_b[l].reshape(MIX // LANE, 1, LANE),
                   gmlp_w_s[l], gmlp_b_s[l][..., None])

        x2 = _merge(x2, z, yh, ya, yg, w_br_hgrn[l].astype(BF16), w_br_attn[l].astype(BF16),
                    w_br_gmlp[l].astype(BF16), w_out[l].astype(BF16))
        x2 = _ffn(x2, norm_ff_w[l].reshape(1, D_MODEL), w_ff1[l].astype(BF16), w_ff2[l].astype(BF16),
                  final_norm_w.reshape(1, D_MODEL), final=(l == depth - 1))
    return x2.reshape(batch, seq, D_MODEL)
```

```python
import functools
import math

import jax
import jax.numpy as jnp
from jax import lax
from jax.experimental import pallas as pl
from jax.experimental.pallas import tpu as pltpu

F32 = jnp.float32
BF16 = jnp.bfloat16

D_MODEL = 1024
MIX = 512
HEADS = 4
DH = 128
DQK = 64
D_FF = 4 * D_MODEL
N_BRANCH = 3
EPS = 1e-6
GMLP_CHUNK = 128
LANE = 128
IN_COLS = 9 * MIX + N_BRANCH * D_MODEL
NG = IN_COLS // LANE

G_GATE = 0
G_HQ, G_HF, G_HI, G_HG = 24, 28, 32, 36
G_AQ, G_AK, G_AV = 40, 44, 48
G_GU, G_GV = 52, 56

LOG2E = 1.4426950408889634
NEG = -1e30
VMEM_LIMIT = 56 * 1024 * 1024

TM_PROJ = 512
T_HGRN = 512
HGRN_HEADS_PER_STEP = 2
SUB = 16
TQ = 512
TK = 512
CH = 256
T_GMLP = 1024
CW = 512


def _cparams(sem):
    return pltpu.CompilerParams(dimension_semantics=sem, vmem_limit_bytes=VMEM_LIMIT)


def _resident(shape, index_map):
    return pl.BlockSpec(shape, index_map, pipeline_mode=pl.Buffered(1))


def _rms(x, w):
    return x * lax.rsqrt(jnp.mean(x * x, axis=-1, keepdims=True) + EPS) * w


def _gelu(x):
    return 0.5 * x * (1.0 + lax.erf(x * (1.0 / math.sqrt(2.0))))


def _dot(a, b):
    return jnp.dot(a, b, preferred_element_type=F32)


def _dot_nt(a, b):
    return lax.dot_general(a, b, (((1,), (1,)), ((), ())), preferred_element_type=F32)


def _dot_tn(a, b):
    return lax.dot_general(a, b, (((0,), (0,)), ((), ())), preferred_element_type=F32)


def _inproj_kernel(x_ref, nw_ref, w_ref, z_ref, h_ref):
    h_ref[...] = _rms(x_ref[...], nw_ref[...]).astype(BF16)
    gpc = CW // LANE
    for jc in range(IN_COLS // CW):
        r = _dot(h_ref[...], w_ref[:, jc * CW:(jc + 1) * CW])
        for g in range(gpc):
            z_ref[jc * gpc + g] = r[:, g * LANE:(g + 1) * LANE].astype(BF16)


def _inproj(x2, nw, w):
    n = x2.shape[0]
    tm = TM_PROJ
    return pl.pallas_call(
        _inproj_kernel,
        grid=(n // tm,),
        in_specs=[
            pl.BlockSpec((tm, D_MODEL), lambda i: (i, 0)),
            _resident((1, D_MODEL), lambda i: (0, 0)),
            _resident((D_MODEL, IN_COLS), lambda i: (0, 0)),
        ],
        out_specs=pl.BlockSpec((NG, tm, LANE), lambda i: (0, i, 0)),
        out_shape=jax.ShapeDtypeStruct((NG, n, LANE), BF16),
        scratch_shapes=[pltpu.VMEM((tm, D_MODEL), BF16)],
        compiler_params=_cparams(("parallel",)),
        name="inproj",
    )(x2, nw, w)


def _sigmoid(z):
    return 0.5 * jnp.tanh(0.5 * z) + 0.5


def _hgrn_kernel(q_ref, f_ref, i_ref, g_ref, llb_ref, l1m_ref, omlb_ref, nw_ref, o_ref, st_ref, tri_ref):
    t_rows = q_ref.shape[1]

    @pl.when(pl.program_id(2) == 0)
    def _():
        st_ref[...] = jnp.zeros_like(st_ref)
        row = lax.broadcasted_iota(jnp.int32, (t_rows, t_rows), 0)
        col = lax.broadcasted_iota(jnp.int32, (t_rows, t_rows), 1)
        tri_ref[...] = jnp.where(row >= col, 1.0, 0.0).astype(BF16)

    for hd in range(q_ref.shape[0]):
        _hgrn_head(hd, q_ref, f_ref, i_ref, g_ref, llb_ref, l1m_ref, omlb_ref, nw_ref, o_ref, st_ref, tri_ref)


def _hgrn_head(hd, q_ref, f_ref, i_ref, g_ref, llb_ref, l1m_ref, omlb_ref, nw_ref, o_ref, st_ref, tri_ref):
    t_rows = q_ref.shape[1]
    chan = slice(hd * DH, (hd + 1) * DH)

    zq = q_ref[hd].astype(F32)
    zf = f_ref[hd].astype(F32)
    vi = i_ref[hd].astype(F32)
    zg = g_ref[hd].astype(F32)

    qf = zq * _sigmoid(zq)
    log_sig = jnp.minimum(zf, 0.0) - jnp.log(1.0 + jnp.exp2(jnp.abs(zf) * (-LOG2E)))
    c = l1m_ref[:, chan] + log_sig
    a = llb_ref[:, chan]
    logf = jnp.maximum(a, c) + jnp.log(1.0 + jnp.exp2(jnp.abs(a - c) * (-LOG2E)))
    kf = omlb_ref[:, chan] * _sigmoid(-zf)
    lf2 = logf * LOG2E

    hi = lf2.astype(BF16)
    r1 = lf2 - hi.astype(F32)
    mid = r1.astype(BF16)
    lo = (r1 - mid.astype(F32)).astype(BF16)
    c3 = _dot(tri_ref[...], jnp.concatenate([hi, mid, lo], axis=1))
    b = c3[:, :DH] + c3[:, DH:2 * DH] + c3[:, 2 * DH:]

    vb = vi.astype(BF16)
    o = jnp.zeros((t_rows, DH), F32)

    h = SUB
    while h < t_rows:
        nb = t_rows // (2 * h)
        b3 = b.reshape(nb, 2 * h, DH)
        ref = b3[:, h:h + 1, :]
        qp = (qf.reshape(nb, 2 * h, DH)[:, h:, :] * jnp.exp2(b3[:, h:, :] - ref)).astype(BF16)
        kp = (kf.reshape(nb, 2 * h, DH)[:, :h, :] * jnp.exp2(ref - b3[:, :h, :])).astype(BF16)
        a_l = jnp.einsum('nqk,nsk->nqs', qp, kp, preferred_element_type=F32)
        o_l = jnp.einsum('nqs,nsv->nqv', a_l.astype(BF16), vb.reshape(nb, 2 * h, DH)[:, :h, :],
                         preferred_element_type=F32)
        o = o + jnp.concatenate([jnp.zeros_like(o_l), o_l], axis=1).reshape(t_rows, DH)
        h *= 2

    nb = t_rows // SUB
    hs = SUB // 2
    q3 = qf.reshape(nb, SUB, DH)
    b3 = b.reshape(nb, SUB, DH)
    c3s = (b - jnp.log2(kf)).reshape(nb, SUB, DH)
    t8 = lax.broadcasted_iota(jnp.int32, (nb, hs, DH), 1)
    lane = lax.broadcasted_iota(jnp.int32, (nb, hs, DH), 2)
    halves = []
    for half in range(2):
        rows = slice(half * hs, (half + 1) * hs)
        qh, bh = q3[:, rows, :], b3[:, rows, :]
        ah = jnp.zeros((nb, hs, DH), F32)
        for s in range((half + 1) * hs):
            d = bh - c3s[:, s:s + 1, :]
            if s >= half * hs:
                d = jnp.where(t8 >= s - half * hs, d, NEG)
            a_s = jnp.sum(qh * jnp.exp2(d), axis=-1, keepdims=True)
            ah = jnp.where(lane == s, a_s, ah)
        halves.append(ah)
    a_blk = jnp.concatenate(halves, axis=1)[:, :, :SUB].astype(BF16)
    o_blk = jnp.einsum('nts,nsv->ntv', a_blk, vb.reshape(nb, SUB, DH), preferred_element_type=F32)
    o = o + o_blk.reshape(t_rows, DH)

    st = st_ref[hd]
    o = o + _dot_nt((qf * jnp.exp2(b)).astype(BF16), st.astype(BF16))
    b_last = b[t_rows - 1:t_rows, :]
    kd = (kf * jnp.exp2(b_last - b)).astype(BF16)
    st_ref[hd] = jnp.exp2(b_last) * st + _dot_tn(vb, kd)

    y = _rms(o, nw_ref[...]) * (zg * _sigmoid(zg))
    o_ref[hd] = y.astype(BF16)


def _hgrn(z, llb, l1m, omlb, nw, batch, seq):
    t = T_HGRN
    nc = seq // t
    n = batch * seq

    hps = HGRN_HEADS_PER_STEP

    def zspec(g0):
        return pl.BlockSpec((hps, t, LANE), lambda b, h, c: (g0 // hps + h, b * nc + c, 0))

    def pspec():
        return pl.BlockSpec((1, hps * LANE), lambda b, h, c: (0, h))

    return pl.pallas_call(
        _hgrn_kernel,
        grid=(batch, HEADS // hps, nc),
        in_specs=[zspec(G_HQ), zspec(G_HF), zspec(G_HI), zspec(G_HG),
                  pspec(), pspec(), pspec(),
                  pl.BlockSpec((1, LANE), lambda b, h, c: (0, 0))],
        out_specs=pl.BlockSpec((hps, t, LANE), lambda b, h, c: (h, b * nc + c, 0)),
        out_shape=jax.ShapeDtypeStruct((HEADS, n, LANE), BF16),
        scratch_shapes=[pltpu.VMEM((hps, DH, DH), F32), pltpu.VMEM((t, t), BF16)],
        compiler_params=_cparams(("parallel", "parallel", "arbitrary")),
        name="hgrn2",
    )(z, z, z, z, llb, l1m, omlb, nw)


def _attn_kernel(lam_ref, q_ref, qn_ref, k_ref, v_ref, nw_ref, o_ref, sa_ref, sb_ref, s0_ref, qts_ref,
                 m_ref, l_ref, acc_ref, *, lam_init):
    qi = pl.program_id(2)
    tq = q_ref.shape[1]

    lp = lam_ref[...]
    lam = (jnp.exp(jnp.sum(lp[0:1] * lp[1:2], axis=-1, keepdims=True))
           - jnp.exp(jnp.sum(lp[2:3] * lp[3:4], axis=-1, keepdims=True)) + lam_init)

    def lane_queries(qblk_ref):
        qt = qblk_ref[0].astype(F32).T * (DQK ** -0.5 * LOG2E)
        sub = lax.broadcasted_iota(jnp.int32, (LANE, tq), 0)
        return jnp.concatenate([jnp.where(sub < DQK, qt, 0.0), jnp.where(sub >= DQK, qt, 0.0)],
                               axis=1).astype(BF16)

    m_ref[...] = jnp.full(m_ref.shape, NEG, F32)
    l_ref[...] = jnp.zeros(l_ref.shape, F32)
    acc_ref[...] = jnp.zeros(acc_ref.shape, F32)

    def stage(jn, sn_ref, jc, cur_ref, masked, qsrc=qts_ref):
        if jn is not None:
            kt = k_ref[0, pl.ds(pl.multiple_of(jn * TK, TK), TK), :]
        if jc is not None:
            voff = pl.multiple_of(jc * TK, TK)
        for c in range(2 * tq // CH):
            ls = slice(c * CH, (c + 1) * CH)
            if jn is not None:
                s = _dot(kt, qsrc[:, ls])
                sn_ref[0:TK, ls] = s
                sn_ref[TK:TK + 1, ls] = jnp.max(s, axis=0, keepdims=True)
            if jc is not None:
                q_lo = (c * CH) % tq
                rows = min(TK, q_lo + CH) if masked else TK
                s = cur_ref[0:rows, ls]
                if masked:
                    krow = lax.broadcasted_iota(jnp.int32, (rows, CH), 0)
                    qcol = lax.broadcasted_iota(jnp.int32, (rows, CH), 1) + q_lo
                    s = jnp.where(krow <= qcol, s, NEG)
                    tile_max = jnp.max(s, axis=0, keepdims=True)
                else:
                    tile_max = cur_ref[TK:TK + 1, ls]
                m_prev = m_ref[:, ls]
                m_cur = jnp.maximum(m_prev, tile_max)
                alpha = jnp.exp2(m_prev - m_cur)
                p = jnp.exp2(s - m_cur)
                l_ref[:, ls] = alpha * l_ref[:, ls] + jnp.sum(p, axis=0, keepdims=True)
                vt = v_ref[0, pl.ds(voff, rows), :]
                acc_ref[:, ls] = alpha * acc_ref[:, ls] + _dot_tn(vt, p.astype(BF16))
                m_ref[:, ls] = m_cur

    def last_stage(jc, cur_ref):
        qn = lane_queries(qn_ref)
        stage(0, s0_ref, jc, cur_ref, True, qsrc=qn)
        qts_ref[...] = qn

    @pl.when(qi == 0)
    def _():
        qts_ref[...] = lane_queries(q_ref)
        stage(0, sa_ref, None, None, False)
        last_stage(0, sa_ref)

    @pl.when(qi > 0)
    def _():
        stage(1, sa_ref, 0, s0_ref, False)

    def pair_body(t, carry):
        stage(2 * t + 2, sb_ref, 2 * t + 1, sa_ref, False)
        stage(2 * t + 3, sa_ref, 2 * t + 2, sb_ref, False)
        return carry

    def quad_body(t, carry):
        pair_body(2 * t, carry)
        pair_body(2 * t + 1, carry)
        return carry

    n_pairs = jnp.maximum(qi - 1, 0) // 2
    lax.fori_loop(0, n_pairs // 2, quad_body, 0)
    lax.fori_loop(2 * (n_pairs // 2), n_pairs, pair_body, 0)

    @pl.when(qi % 2 == 1)
    def _():
        last_stage(qi, sa_ref)

    @pl.when((qi % 2 == 0) & (qi > 0))
    def _():
        stage(qi, sb_ref, qi - 1, sa_ref, False)
        last_stage(qi, sb_ref)

    on = acc_ref[...] / l_ref[...]
    ot = on[:, :tq] - lam * on[:, tq:]
    yt = ot * lax.rsqrt(jnp.mean(ot * ot, axis=0, keepdims=True) + EPS) * nw_ref[...]
    o_ref[0] = (yt.T * (1.0 - lam_init)).astype(BF16)


def _attn(z, lam_params, nw, lam_init, batch, seq):
    assert TQ == TK
    nq = seq // TQ
    n = batch * seq
    return pl.pallas_call(
        functools.partial(_attn_kernel, lam_init=lam_init),
        grid=(batch, HEADS, nq),
        in_specs=[
            pl.BlockSpec((4, DQK), lambda b, h, i: (0, 0)),
            pl.BlockSpec((1, TQ, LANE), lambda b, h, i: (G_AQ + h, b * nq + i, 0)),
            pl.BlockSpec((1, TQ, LANE), lambda b, h, i: (G_AQ + h, b * nq + jnp.minimum(i + 1, nq - 1), 0)),
            pl.BlockSpec((1, seq, LANE), lambda b, h, i: (G_AK + h, b, 0)),
            pl.BlockSpec((1, seq, LANE), lambda b, h, i: (G_AV + h, b, 0)),
            pl.BlockSpec((DH, 1), lambda b, h, i: (0, 0)),
        ],
        out_specs=pl.BlockSpec((1, TQ, LANE), lambda b, h, i: (h, b * nq + i, 0)),
        out_shape=jax.ShapeDtypeStruct((HEADS, n, LANE), BF16),
        scratch_shapes=[pltpu.VMEM((TK + 8, 2 * TQ), F32), pltpu.VMEM((TK + 8, 2 * TQ), F32),
                        pltpu.VMEM((TK + 8, 2 * TQ), F32), pltpu.VMEM((DH, 2 * TQ), BF16),
                        pltpu.VMEM((1, 2 * TQ), F32), pltpu.VMEM((1, 2 * TQ), F32),
                        pltpu.VMEM((DH, 2 * TQ), F32)],
        compiler_params=_cparams(("arbitrary", "arbitrary", "arbitrary")),
        name="diff_attn",
    )(lam_params, z, z, z, z, nw)


def _gmlp_kernel(u_ref, v_ref, lnw_ref, lnb_ref, ws_ref, bs_ref, o_ref):
    t_rows = u_ref.shape[1]
    ng = MIX // LANE
    vs = [_gelu(v_ref[g].astype(F32)) for g in range(ng)]
    mu = sum(jnp.sum(v, axis=-1, keepdims=True) for v in vs) * (1.0 / MIX)
    var = sum(jnp.sum(jnp.square(v - mu), axis=-1, keepdims=True) for v in vs) * (1.0 / MIX)
    rs = lax.rsqrt(var + EPS)
    row = lax.broadcasted_iota(jnp.int32, (GMLP_CHUNK, GMLP_CHUNK), 0)
    col = lax.broadcasted_iota(jnp.int32, (GMLP_CHUNK, GMLP_CHUNK), 1)
    for g in range(ng):
        vn = ((vs[g] - mu) * rs * lnw_ref[g] + lnb_ref[g]).astype(BF16)
        w = jnp.where(row >= col, ws_ref[g], 0.0).astype(BF16)
        for n in range(t_rows // GMLP_CHUNK):
            sl = slice(n * GMLP_CHUNK, (n + 1) * GMLP_CHUNK)
            mixed = _dot(w, vn[sl]) + bs_ref[g]
            o_ref[g, sl, :] = (_gelu(u_ref[g, sl, :].astype(F32)) * mixed).astype(BF16)


def _gmlp(z, lnw, lnb, ws, bs):
    n = z.shape[1]
    t = T_GMLP
    ng = MIX // LANE
    return pl.pallas_call(
        _gmlp_kernel,
        grid=(n // t,),
        in_specs=[
            pl.BlockSpec((ng, t, LANE), lambda i: (G_GU // ng, i, 0)),
            pl.BlockSpec((ng, t, LANE), lambda i: (G_GV // ng, i, 0)),
            pl.BlockSpec((ng, 1, LANE), lambda i: (0, 0, 0)),
            pl.BlockSpec((ng, 1, LANE), lambda i: (0, 0, 0)),
            pl.BlockSpec((ng, GMLP_CHUNK, GMLP_CHUNK), lambda i: (0, 0, 0)),
            pl.BlockSpec((ng, GMLP_CHUNK, 1), lambda i: (0, 0, 0)),
        ],
        out_specs=pl.BlockSpec((ng, t, LANE), lambda i: (0, i, 0)),
        out_shape=jax.ShapeDtypeStruct((ng, n, LANE), BF16),
        compiler_params=_cparams(("parallel",)),
        name="gmlp",
    )(z, z, lnw, lnb, ws, bs)


def _merge_kernel(x_ref, gl_ref, yh_ref, ya_ref, yg_ref, wh_ref, wa_ref, wg_ref, wo_ref, o_ref, m_ref):
    ng = MIX // LANE
    ys = [jnp.concatenate([r[g] for g in range(ng)], axis=-1) for r in (yh_ref, ya_ref, yg_ref)]
    ws = (wh_ref, wa_ref, wg_ref)
    cw = 2 * LANE
    gpb = D_MODEL // LANE
    for cc in range(D_MODEL // cw):
        acc = None
        for br in range(N_BRANCH):
            p = _dot(ys[br], ws[br][:, cc * cw:(cc + 1) * cw])
            g0 = br * gpb + 2 * cc
            gl = jnp.concatenate([gl_ref[g0], gl_ref[g0 + 1]], axis=-1).astype(F32)
            t = _sigmoid(gl) * p
            acc = t if acc is None else acc + t
        m_ref[:, cc * cw:(cc + 1) * cw] = acc.astype(BF16)
    o_ref[...] = x_ref[...] + _dot(m_ref[...], wo_ref[...])


def _merge(x2, z, yh, ya, yg, wh, wa, wg, wo):
    n = x2.shape[0]
    tm = TM_PROJ
    ng = MIX // LANE
    ngate = N_BRANCH * D_MODEL // LANE

    def yspec():
        return pl.BlockSpec((ng, tm, LANE), lambda i: (0, i, 0))

    def wspec():
        return _resident((MIX, D_MODEL), lambda i: (0, 0))

    return pl.pallas_call(
        _merge_kernel,
        grid=(n // tm,),
        in_specs=[
            pl.BlockSpec((tm, D_MODEL), lambda i: (i, 0)),
            pl.BlockSpec((ngate, tm, LANE), lambda i: (G_GATE // ngate, i, 0)),
            yspec(), yspec(), yspec(),
            wspec(), wspec(), wspec(),
            _resident((D_MODEL, D_MODEL), lambda i: (0, 0)),
        ],
        out_specs=pl.BlockSpec((tm, D_MODEL), lambda i: (i, 0)),
        out_shape=jax.ShapeDtypeStruct((n, D_MODEL), F32),
        scratch_shapes=[pltpu.VMEM((tm, D_MODEL), BF16)],
        compiler_params=_cparams(("parallel",)),
        name="merge_out",
    )(x2, z, yh, ya, yg, wh, wa, wg, wo)


def _ffn_kernel(x_ref, nw_ref, w1_ref, w2_ref, fw_ref, o_ref, h_ref, ff_ref, *, final):
    x = x_ref[...]
    h_ref[...] = _rms(x, nw_ref[...]).astype(BF16)
    fc = D_MODEL
    for j in range(D_FF // fc):
        a = jnp.maximum(_dot(h_ref[...], w1_ref[:, j * fc:(j + 1) * fc]), 0.0)
        ff_ref[:, j * fc:(j + 1) * fc] = (a * a).astype(BF16)
    y = x + _dot(ff_ref[...], w2_ref[...])
    if final:
        y = _rms(y, fw_ref[...])
    o_ref[...] = y


def _ffn(x2, nw, w1, w2, fw, final):
    n = x2.shape[0]
    tm = TM_PROJ
    return pl.pallas_call(
        functools.partial(_ffn_kernel, final=final),
        grid=(n // tm,),
        in_specs=[
            pl.BlockSpec((tm, D_MODEL), lambda i: (i, 0)),
            _resident((1, D_MODEL), lambda i: (0, 0)),
            _resident((D_MODEL, D_FF), lambda i: (0, 0)),
            _resident((D_FF, D_MODEL), lambda i: (0, 0)),
            _resident((1, D_MODEL), lambda i: (0, 0)),
        ],
        out_specs=pl.BlockSpec((tm, D_MODEL), lambda i: (i, 0)),
        out_shape=jax.ShapeDtypeStruct((n, D_MODEL), F32),
        scratch_shapes=[pltpu.VMEM((tm, D_MODEL), BF16), pltpu.VMEM((tm, D_FF), BF16)],
        compiler_params=_cparams(("parallel",)),
        name="ffn",
    )(x2, nw, w1, w2, fw)


def kernel(x, norm_mix_w, w_in, hgrn_lb_logits, hgrn_norm_w, diff_lam_q1, diff_lam_k1, diff_lam_q2,
           diff_lam_k2, diff_norm_w, gmlp_ln_w, gmlp_ln_b, gmlp_w_s, gmlp_b_s, w_br_hgrn, w_br_attn,
           w_br_gmlp, w_out, norm_ff_w, w_ff1, w_ff2, final_norm_w):
    batch, seq, _ = x.shape
    depth = w_in.shape[0]
    n = batch * seq
    assert n % TM_PROJ == 0 and seq % T_HGRN == 0 and seq % TQ == 0 and n % T_GMLP == 0

    cum = jnp.cumsum(jax.nn.softmax(hgrn_lb_logits.astype(F32), axis=0), axis=0)
    lbs = cum - cum[0:1]
    log_lb = jnp.maximum(jnp.log(lbs), NEG)
    log_1m_lb = jnp.log1p(-lbs)
    one_m_lb = 1.0 - lbs

    split = 9 * MIX
    x2 = x.reshape(n, D_MODEL)
    for l in range(depth):
        w_l = jnp.concatenate([w_in[l][:, split:], w_in[l][:, :split]], axis=1).astype(BF16)
        z = _inproj(x2, norm_mix_w[l].reshape(1, D_MODEL), w_l)

        yh = _hgrn(z, log_lb[l].reshape(1, MIX), log_1m_lb[l].reshape(1, MIX),
                   one_m_lb[l].reshape(1, MIX), hgrn_norm_w[l].reshape(1, DH), batch, seq)
        lam_init = 0.8 - 0.6 * math.exp(-0.3 * l)
        lam_params = jnp.stack([diff_lam_q1[l], diff_lam_k1[l], diff_lam_q2[l], diff_lam_k2[l]]).astype(F32)
        ya = _attn(z, lam_params, diff_norm_w[l].reshape(DH, 1), lam_init, batch, seq)
        yg = _gmlp(z, gmlp_ln_w[l].reshape(MIX // LANE, 1, LANE), gmlp_ln_b[l].reshape(MIX // LANE, 1, LANE),
                   gmlp_w_s[l], gmlp_b_s[l][..., None])

        x2 = _merge(x2, z, yh, ya, yg, w_br_hgrn[l].astype(BF16), w_br_attn[l].astype(BF16),
                    w_br_gmlp[l].astype(BF16), w_out[l].astype(BF16))
        x2 = _ffn(x2, norm_ff_w[l].reshape(1, D_MODEL), w_ff1[l].astype(BF16), w_ff2[l].astype(BF16),
                  final_norm_w.reshape(1, D_MODEL), final=(l == depth - 1))
    return x2.reshape(batch, seq, D_MODEL)
```

```python
import functools
import math

import jax
import jax.numpy as jnp
from jax import lax
from jax.experimental import pallas as pl
from jax.experimental.pallas import tpu as pltpu

F32 = jnp.float32
BF16 = jnp.bfloat16

D_MODEL = 1024
MIX = 512
HEADS = 4
DH = 128
DQK = 64
D_FF = 4 * D_MODEL
N_BRANCH = 3
EPS = 1e-6
GMLP_CHUNK = 128
LANE = 128
IN_COLS = 9 * MIX + N_BRANCH * D_MODEL
NG = IN_COLS // LANE

G_GATE = 0
G_HQ, G_HF, G_HI, G_HG = 24, 28, 32, 36
G_AQ, G_AK, G_AV = 40, 44, 48
G_GU, G_GV = 52, 56

LOG2E = 1.4426950408889634
NEG = -1e30
VMEM_LIMIT = 56 * 1024 * 1024

TM_PROJ = 512
T_HGRN = 512
HGRN_HEADS_PER_STEP = 4
SUB = 16
TQ = 512
TK = 512
CH = 256
T_GMLP = 1024
CW = 512


def _cparams(sem):
    return pltpu.CompilerParams(dimension_semantics=sem, vmem_limit_bytes=VMEM_LIMIT)


def _resident(shape, index_map):
    return pl.BlockSpec(shape, index_map, pipeline_mode=pl.Buffered(1))


def _rms(x, w):
    return x * lax.rsqrt(jnp.mean(x * x, axis=-1, keepdims=True) + EPS) * w


def _gelu(x):
    return 0.5 * x * (1.0 + lax.erf(x * (1.0 / math.sqrt(2.0))))


def _dot(a, b):
    return jnp.dot(a, b, preferred_element_type=F32)


def _dot_nt(a, b):
    return lax.dot_general(a, b, (((1,), (1,)), ((), ())), preferred_element_type=F32)


def _dot_tn(a, b):
    return lax.dot_general(a, b, (((0,), (0,)), ((), ())), preferred_element_type=F32)


def _inproj_kernel(x_ref, nw_ref, w_ref, z_ref, h_ref):
    h_ref[...] = _rms(x_ref[...], nw_ref[...]).astype(BF16)
    gpc = CW // LANE
    for jc in range(IN_COLS // CW):
        r = _dot(h_ref[...], w_ref[:, jc * CW:(jc + 1) * CW])
        for g in range(gpc):
            z_ref[jc * gpc + g] = r[:, g * LANE:(g + 1) * LANE].astype(BF16)


def _inproj(x2, nw, w):
    n = x2.shape[0]
    tm = TM_PROJ
    return pl.pallas_call(
        _inproj_kernel,
        grid=(n // tm,),
        in_specs=[
            pl.BlockSpec((tm, D_MODEL), lambda i: (i, 0)),
            _resident((1, D_MODEL), lambda i: (0, 0)),
            _resident((D_MODEL, IN_COLS), lambda i: (0, 0)),
        ],
        out_specs=pl.BlockSpec((NG, tm, LANE), lambda i: (0, i, 0)),
        out_shape=jax.ShapeDtypeStruct((NG, n, LANE), BF16),
        scratch_shapes=[pltpu.VMEM((tm, D_MODEL), BF16)],
        compiler_params=_cparams(("parallel",)),
        name="inproj",
    )(x2, nw, w)


def _sigmoid(z):
    return 0.5 * jnp.tanh(0.5 * z) + 0.5


def _hgrn_kernel(q_ref, f_ref, i_ref, g_ref, llb_ref, l1m_ref, omlb_ref, nw_ref, o_ref, st_ref, tri_ref):
    t_rows = q_ref.shape[1]

    @pl.when(pl.program_id(2) == 0)
    def _():
        st_ref[...] = jnp.zeros_like(st_ref)
        row = lax.broadcasted_iota(jnp.int32, (t_rows, t_rows), 0)
        col = lax.broadcasted_iota(jnp.int32, (t_rows, t_rows), 1)
        tri_ref[...] = jnp.where(row >= col, 1.0, 0.0).astype(BF16)

    for hd in range(q_ref.shape[0]):
        _hgrn_head(hd, q_ref, f_ref, i_ref, g_ref, llb_ref, l1m_ref, omlb_ref, nw_ref, o_ref, st_ref, tri_ref)


def _hgrn_head(hd, q_ref, f_ref, i_ref, g_ref, llb_ref, l1m_ref, omlb_ref, nw_ref, o_ref, st_ref, tri_ref):
    t_rows = q_ref.shape[1]
    chan = slice(hd * DH, (hd + 1) * DH)

    zq = q_ref[hd].astype(F32)
    zf = f_ref[hd].astype(F32)
    vi = i_ref[hd].astype(F32)
    zg = g_ref[hd].astype(F32)

    qf = zq * _sigmoid(zq)
    log_sig = jnp.minimum(zf, 0.0) - jnp.log(1.0 + jnp.exp2(jnp.abs(zf) * (-LOG2E)))
    c = l1m_ref[:, chan] + log_sig
    a = llb_ref[:, chan]
    logf = jnp.maximum(a, c) + jnp.log(1.0 + jnp.exp2(jnp.abs(a - c) * (-LOG2E)))
    kf = omlb_ref[:, chan] * _sigmoid(-zf)
    lf2 = logf * LOG2E

    hi = lf2.astype(BF16)
    r1 = lf2 - hi.astype(F32)
    mid = r1.astype(BF16)
    lo = (r1 - mid.astype(F32)).astype(BF16)
    c3 = _dot(tri_ref[...], jnp.concatenate([hi, mid, lo], axis=1))
    b = c3[:, :DH] + c3[:, DH:2 * DH] + c3[:, 2 * DH:]

    vb = vi.astype(BF16)
    o = jnp.zeros((t_rows, DH), F32)

    h = SUB
    while h < t_rows:
        nb = t_rows // (2 * h)
        b3 = b.reshape(nb, 2 * h, DH)
        ref = b3[:, h:h + 1, :]
        qp = (qf.reshape(nb, 2 * h, DH)[:, h:, :] * jnp.exp2(b3[:, h:, :] - ref)).astype(BF16)
        kp = (kf.reshape(nb, 2 * h, DH)[:, :h, :] * jnp.exp2(ref - b3[:, :h, :])).astype(BF16)
        a_l = jnp.einsum('nqk,nsk->nqs', qp, kp, preferred_element_type=F32)
        o_l = jnp.einsum('nqs,nsv->nqv', a_l.astype(BF16), vb.reshape(nb, 2 * h, DH)[:, :h, :],
                         preferred_element_type=F32)
        o = o + jnp.concatenate([jnp.zeros_like(o_l), o_l], axis=1).reshape(t_rows, DH)
        h *= 2

    nb = t_rows // SUB
    hs = SUB // 2
    q3 = qf.reshape(nb, SUB, DH)
    b3 = b.reshape(nb, SUB, DH)
    c3s = (b - jnp.log2(kf)).reshape(nb, SUB, DH)
    t8 = lax.broadcasted_iota(jnp.int32, (nb, hs, DH), 1)
    lane = lax.broadcasted_iota(jnp.int32, (nb, hs, DH), 2)
    halves = []
    for half in range(2):
        rows = slice(half * hs, (half + 1) * hs)
        qh, bh = q3[:, rows, :], b3[:, rows, :]
        ah = jnp.zeros((nb, hs, DH), F32)
        for s in range((half + 1) * hs):
            d = bh - c3s[:, s:s + 1, :]
            if s >= half * hs:
                d = jnp.where(t8 >= s - half * hs, d, NEG)
            a_s = jnp.sum(qh * jnp.exp2(d), axis=-1, keepdims=True)
            ah = jnp.where(lane == s, a_s, ah)
        halves.append(ah)
    a_blk = jnp.concatenate(halves, axis=1)[:, :, :SUB].astype(BF16)
    o_blk = jnp.einsum('nts,nsv->ntv', a_blk, vb.reshape(nb, SUB, DH), preferred_element_type=F32)
    o = o + o_blk.reshape(t_rows, DH)

    st = st_ref[hd]
    o = o + _dot_nt((qf * jnp.exp2(b)).astype(BF16), st.astype(BF16))
    b_last = b[t_rows - 1:t_rows, :]
    kd = (kf * jnp.exp2(b_last - b)).astype(BF16)
    st_ref[hd] = jnp.exp2(b_last) * st + _dot_tn(vb, kd)

    y = _rms(o, nw_ref[...]) * (zg * _sigmoid(zg))
    o_ref[hd] = y.astype(BF16)


def _hgrn(z, llb, l1m, omlb, nw, batch, seq):
    t = T_HGRN
    nc = seq // t
    n = batch * seq

    hps = HGRN_HEADS_PER_STEP

    def zspec(g0):
        return pl.BlockSpec((hps, t, LANE), lambda b, h, c: (g0 // hps + h, b * nc + c, 0))

    def pspec():
        return pl.BlockSpec((1, hps * LANE), lambda b, h, c: (0, h))

    return pl.pallas_call(
        _hgrn_kernel,
        grid=(batch, HEADS // hps, nc),
        in_specs=[zspec(G_HQ), zspec(G_HF), zspec(G_HI), zspec(G_HG),
                  pspec(), pspec(), pspec(),
                  pl.BlockSpec((1, LANE), lambda b, h, c: (0, 0))],
        out_specs=pl.BlockSpec((hps, t, LANE), lambda b, h, c: (h, b * nc + c, 0)),
        out_shape=jax.ShapeDtypeStruct((HEADS, n, LANE), BF16),
        scratch_shapes=[pltpu.VMEM((hps, DH, DH), F32), pltpu.VMEM((t, t), BF16)],
        compiler_params=_cparams(("parallel", "parallel", "arbitrary")),
        name="hgrn2",
    )(z, z, z, z, llb, l1m, omlb, nw)


def _attn_kernel(lam_ref, q_ref, k_ref, v_ref, nw_ref, o_ref, sa_ref, sb_ref, m_ref, l_ref, acc_ref, *,
                 lam_init):
    qi = pl.program_id(2)
    tq = q_ref.shape[1]

    lp = lam_ref[...]
    lam = (jnp.exp(jnp.sum(lp[0:1] * lp[1:2], axis=-1, keepdims=True))
           - jnp.exp(jnp.sum(lp[2:3] * lp[3:4], axis=-1, keepdims=True)) + lam_init)

    qt = q_ref[0].astype(F32).T * (DQK ** -0.5 * LOG2E)
    sub = lax.broadcasted_iota(jnp.int32, (LANE, tq), 0)
    qts = jnp.concatenate([jnp.where(sub < DQK, qt, 0.0), jnp.where(sub >= DQK, qt, 0.0)],
                          axis=1).astype(BF16)

    m_ref[...] = jnp.full(m_ref.shape, NEG, F32)
    l_ref[...] = jnp.zeros(l_ref.shape, F32)
    acc_ref[...] = jnp.zeros(acc_ref.shape, F32)

    def stage(jn, sn_ref, jc, sc_ref, masked):
        if jn is not None:
            kt = k_ref[0, pl.ds(pl.multiple_of(jn * TK, TK), TK), :]
        if jc is not None:
            vt = v_ref[0, pl.ds(pl.multiple_of(jc * TK, TK), TK), :]
        for c in range(2 * tq // CH):
            ls = slice(c * CH, (c + 1) * CH)
            if jn is not None:
                s = _dot(kt, qts[:, ls])
                sn_ref[0:TK, ls] = s
                sn_ref[TK:TK + 1, ls] = jnp.max(s, axis=0, keepdims=True)
            if jc is not None:
                s = sc_ref[0:TK, ls]
                if masked:
                    krow = lax.broadcasted_iota(jnp.int32, (TK, CH), 0)
                    qcol = (lax.broadcasted_iota(jnp.int32, (TK, CH), 1) + c * CH) & (tq - 1)
                    s = jnp.where(krow <= qcol, s, NEG)
                    tile_max = jnp.max(s, axis=0, keepdims=True)
                else:
                    tile_max = sc_ref[TK:TK + 1, ls]
                m_prev = m_ref[:, ls]
                m_cur = jnp.maximum(m_prev, tile_max)
                alpha = jnp.exp2(m_prev - m_cur)
                p = jnp.exp2(s - m_cur)
                l_ref[:, ls] = alpha * l_ref[:, ls] + jnp.sum(p, axis=0, keepdims=True)
                acc_ref[:, ls] = alpha * acc_ref[:, ls] + _dot_tn(vt, p.astype(BF16))
                m_ref[:, ls] = m_cur

    stage(0, sa_ref, None, None, False)

    def pair_body(t, carry):
        stage(2 * t + 1, sb_ref, 2 * t, sa_ref, False)
        stage(2 * t + 2, sa_ref, 2 * t + 1, sb_ref, False)
        return carry

    def quad_body(t, carry):
        pair_body(2 * t, carry)
        pair_body(2 * t + 1, carry)
        return carry

    def oct_body(t, carry):
        quad_body(2 * t, carry)
        quad_body(2 * t + 1, carry)
        return carry

    lax.fori_loop(0, qi // 8, oct_body, 0)
    lax.fori_loop(2 * (qi // 8), qi // 4, quad_body, 0)
    lax.fori_loop(2 * (qi // 4), qi // 2, pair_body, 0)

    @pl.when(qi % 2 == 0)
    def _():
        stage(None, None, qi, sa_ref, True)

    @pl.when(qi % 2 == 1)
    def _():
        stage(qi, sb_ref, qi - 1, sa_ref, False)
        stage(None, None, qi, sb_ref, True)

    on = acc_ref[...] / l_ref[...]
    ot = on[:, :tq] - lam * on[:, tq:]
    yt = ot * lax.rsqrt(jnp.mean(ot * ot, axis=0, keepdims=True) + EPS) * nw_ref[...]
    o_ref[0] = (yt.T * (1.0 - lam_init)).astype(BF16)


def _attn(z, lam_params, nw, lam_init, batch, seq):
    assert TQ == TK
    nq = seq // TQ
    n = batch * seq
    return pl.pallas_call(
        functools.partial(_attn_kernel, lam_init=lam_init),
        grid=(batch, HEADS, nq),
        in_specs=[
            pl.BlockSpec((4, DQK), lambda b, h, i: (0, 0)),
            pl.BlockSpec((1, TQ, LANE), lambda b, h, i: (G_AQ + h, b * nq + i, 0)),
            pl.BlockSpec((1, seq, LANE), lambda b, h, i: (G_AK + h, b, 0)),
            pl.BlockSpec((1, seq, LANE), lambda b, h, i: (G_AV + h, b, 0)),
            pl.BlockSpec((DH, 1), lambda b, h, i: (0, 0)),
        ],
        out_specs=pl.BlockSpec((1, TQ, LANE), lambda b, h, i: (h, b * nq + i, 0)),
        out_shape=jax.ShapeDtypeStruct((HEADS, n, LANE), BF16),
        scratch_shapes=[pltpu.VMEM((TK + 8, 2 * TQ), F32), pltpu.VMEM((TK + 8, 2 * TQ), F32),
                        pltpu.VMEM((1, 2 * TQ), F32), pltpu.VMEM((1, 2 * TQ), F32),
                        pltpu.VMEM((DH, 2 * TQ), F32)],
        compiler_params=_cparams(("parallel", "parallel", "arbitrary")),
        name="diff_attn",
    )(lam_params, z, z, z, nw)


def _gmlp_kernel(u_ref, v_ref, lnw_ref, lnb_ref, ws_ref, bs_ref, o_ref):
    t_rows = u_ref.shape[1]
    ng = MIX // LANE
    vs = [_gelu(v_ref[g].astype(F32)) for g in range(ng)]
    mu = sum(jnp.sum(v, axis=-1, keepdims=True) for v in vs) * (1.0 / MIX)
    var = sum(jnp.sum(jnp.square(v - mu), axis=-1, keepdims=True) for v in vs) * (1.0 / MIX)
    rs = lax.rsqrt(var + EPS)
    row = lax.broadcasted_iota(jnp.int32, (GMLP_CHUNK, GMLP_CHUNK), 0)
    col = lax.broadcasted_iota(jnp.int32, (GMLP_CHUNK, GMLP_CHUNK), 1)
    for g in range(ng):
        vn = ((vs[g] - mu) * rs * lnw_ref[g] + lnb_ref[g]).astype(BF16)
        w = jnp.where(row >= col, ws_ref[g], 0.0).astype(BF16)
        for n in range(t_rows // GMLP_CHUNK):
            sl = slice(n * GMLP_CHUNK, (n + 1) * GMLP_CHUNK)
            mixed = _dot(w, vn[sl]) + bs_ref[g]
            o_ref[g, sl, :] = (_gelu(u_ref[g, sl, :].astype(F32)) * mixed).astype(BF16)


def _gmlp(z, lnw, lnb, ws, bs):
    n = z.shape[1]
    t = T_GMLP
    ng = MIX // LANE
    return pl.pallas_call(
        _gmlp_kernel,
        grid=(n // t,),
        in_specs=[
            pl.BlockSpec((ng, t, LANE), lambda i: (G_GU // ng, i, 0)),
            pl.BlockSpec((ng, t, LANE), lambda i: (G_GV // ng, i, 0)),
            pl.BlockSpec((ng, 1, LANE), lambda i: (0, 0, 0)),
            pl.BlockSpec((ng, 1, LANE), lambda i: (0, 0, 0)),
            pl.BlockSpec((ng, GMLP_CHUNK, GMLP_CHUNK), lambda i: (0, 0, 0)),
            pl.BlockSpec((ng, GMLP_CHUNK, 1), lambda i: (0, 0, 0)),
        ],
        out_specs=pl.BlockSpec((ng, t, LANE), lambda i: (0, i, 0)),
        out_shape=jax.ShapeDtypeStruct((ng, n, LANE), BF16),
        compiler_params=_cparams(("parallel",)),
        name="gmlp",
    )(z, z, lnw, lnb, ws, bs)


def _merge_kernel(x_ref, gl_ref, yh_ref, ya_ref, yg_ref, wh_ref, wa_ref, wg_ref, wo_ref, o_ref, m_ref):
    ng = MIX // LANE
    ys = [jnp.concatenate([r[g] for g in range(ng)], axis=-1) for r in (yh_ref, ya_ref, yg_ref)]
    ws = (wh_ref, wa_ref, wg_ref)
    cw = 2 * LANE
    gpb = D_MODEL // LANE
    for cc in range(D_MODEL // cw):
        acc = None
        for br in range(N_BRANCH):
            p = _dot(ys[br], ws[br][:, cc * cw:(cc + 1) * cw])
            g0 = br * gpb + 2 * cc
            gl = jnp.concatenate([gl_ref[g0], gl_ref[g0 + 1]], axis=-1).astype(F32)
            t = _sigmoid(gl) * p
            acc = t if acc is None else acc + t
        m_ref[:, cc * cw:(cc + 1) * cw] = acc.astype(BF16)
    o_ref[...] = x_ref[...] + _dot(m_ref[...], wo_ref[...])


def _merge(x2, z, yh, ya, yg, wh, wa, wg, wo):
    n = x2.shape[0]
    tm = TM_PROJ
    ng = MIX // LANE
    ngate = N_BRANCH * D_MODEL // LANE

    def yspec():
        return pl.BlockSpec((ng, tm, LANE), lambda i: (0, i, 0))

    def wspec():
        return _resident((MIX, D_MODEL), lambda i: (0, 0))

    return pl.pallas_call(
        _merge_kernel,
        grid=(n // tm,),
        in_specs=[
            pl.BlockSpec((tm, D_MODEL), lambda i: (i, 0)),
            pl.BlockSpec((ngate, tm, LANE), lambda i: (G_GATE // ngate, i, 0)),
            yspec(), yspec(), yspec(),
            wspec(), wspec(), wspec(),
            _resident((D_MODEL, D_MODEL), lambda i: (0, 0)),
        ],
        out_specs=pl.BlockSpec((tm, D_MODEL), lambda i: (i, 0)),
        out_shape=jax.ShapeDtypeStruct((n, D_MODEL), F32),
        scratch_shapes=[pltpu.VMEM((tm, D_MODEL), BF16)],
        compiler_params=_cparams(("parallel",)),
        name="merge_out",
    )(x2, z, yh, ya, yg, wh, wa, wg, wo)


def _ffn_kernel(x_ref, nw_ref, w1_ref, w2_ref, fw_ref, o_ref, h_ref, ff_ref, *, final):
    x = x_ref[...]
    h_ref[...] = _rms(x, nw_ref[...]).astype(BF16)
    fc = D_MODEL
    for j in range(D_FF // fc):
        a = jnp.maximum(_dot(h_ref[...], w1_ref[:, j * fc:(j + 1) * fc]), 0.0)
        ff_ref[:, j * fc:(j + 1) * fc] = (a * a).astype(BF16)
    y = x + _dot(ff_ref[...], w2_ref[...])
    if final:
        y = _rms(y, fw_ref[...])
    o_ref[...] = y


def _ffn(x2, nw, w1, w2, fw, final):
    n = x2.shape[0]
    tm = TM_PROJ
    return pl.pallas_call(
        functools.partial(_ffn_kernel, final=final),
        grid=(n // tm,),
        in_specs=[
            pl.BlockSpec((tm, D_MODEL), lambda i: (i, 0)),
            _resident((1, D_MODEL), lambda i: (0, 0)),
            _resident((D_MODEL, D_FF), lambda i: (0, 0)),
            _resident((D_FF, D_MODEL), lambda i: (0, 0)),
            _resident((1, D_MODEL), lambda i: (0, 0)),
        ],
        out_specs=pl.BlockSpec((tm, D_MODEL), lambda i: (i, 0)),
        out_shape=jax.ShapeDtypeStruct((n, D_MODEL), F32),
        scratch_shapes=[pltpu.VMEM((tm, D_MODEL), BF16), pltpu.VMEM((tm, D_FF), BF16)],
        compiler_params=_cparams(("parallel",)),
        name="ffn",
    )(x2, nw, w1, w2, fw)


def kernel(x, norm_mix_w, w_in, hgrn_lb_logits, hgrn_norm_w, diff_lam_q1, diff_lam_k1, diff_lam_q2,
           diff_lam_k2, diff_norm_w, gmlp_ln_w, gmlp_ln_b, gmlp_w_s, gmlp_b_s, w_br_hgrn, w_br_attn,
           w_br_gmlp, w_out, norm_ff_w, w_ff1, w_ff2, final_norm_w):
    batch, seq, _ = x.shape
    depth = w_in.shape[0]
    n = batch * seq
    assert n % TM_PROJ == 0 and seq % T_HGRN == 0 and seq % TQ == 0 and n % T_GMLP == 0

    cum = jnp.cumsum(jax.nn.softmax(hgrn_lb_logits.astype(F32), axis=0), axis=0)
    lbs = cum - cum[0:1]
    log_lb = jnp.maximum(jnp.log(lbs), NEG)
    log_1m_lb = jnp.log1p(-lbs)
    one_m_lb = 1.0 - lbs

    split = 9 * MIX
    x2 = x.reshape(n, D_MODEL)
    for l in range(depth):
        w_l = jnp.concatenate([w_in[l][:, split:], w_in[l][:, :split]], axis=1).astype(BF16)
        z = _inproj(x2, norm_mix_w[l].reshape(1, D_MODEL), w_l)

        yh = _hgrn(z, log_lb[l].reshape(1, MIX), log_1m_lb[l].reshape(1, MIX),
                   one_m_lb[l].reshape(1, MIX), hgrn_norm_w[l].reshape(1, DH), batch, seq)
        lam_init = 0.8 - 0.6 * math.exp(-0.3 * l)
        lam_params = jnp.stack([diff_lam_q1[l], diff_lam_k1[l], diff_lam_q2[l], diff_lam_k2[l]]).astype(F32)
        ya = _attn(z, lam_params, diff_norm_w[l].reshape(DH, 1), lam_init, batch, seq)
        yg = _gmlp(z, gmlp_ln_w[l].reshape(MIX // LANE, 1, LANE), gmlp_ln_b[l].reshape(MIX // LANE, 1, LANE),
                   gmlp_w_s[l], gmlp_b_s[l][..., None])

        x2 = _merge(x2, z, yh, ya, yg, w_br_hgrn[l].astype(BF16), w_br_attn[l].astype(BF16),
                    w_br_gmlp[l].astype(BF16), w_out[l].astype(BF16))
        x2 = _ffn(x2, norm_ff_w[l].reshape(1, D_MODEL), w_ff1[l].astype(BF16), w_ff2[l].astype(BF16),
                  final_norm_w.reshape(1, D_MODEL), final=(l == depth - 1))
    return x2.reshape(batch, seq, D_MODEL)
```
